```python
import math
import jax, jax.numpy as jnp
from jax import lax
import numpy as np

D_MODEL = 1024
BATCH = 2
SEQ = 8192
DEPTH = 2

GRID_W = 64
CTX_LEN = 256
N_MIXERS = 2
N_SSD_LAYERS = (DEPTH + 1) // 2
N_ATTN_LAYERS = DEPTH // 2
ALPHA = (2.0 * DEPTH) ** 0.25
BETA = (8.0 * DEPTH) ** -0.25
LN_EPS = 1e-5
RMS_EPS = 1e-5

SSD_EXPAND = 2
D_INNER = SSD_EXPAND * D_MODEL
SSD_HEAD_DIM = 64
SSD_HEADS = D_INNER // SSD_HEAD_DIM
SSD_GROUPS = 4
SSD_HEADS_PER_GROUP = SSD_HEADS // SSD_GROUPS
D_STATE = 128
CONV_W = 3
CONV_DIM = D_INNER + 2 * SSD_GROUPS * D_STATE
SSD_IN_DIM = D_INNER + CONV_DIM + 2 * SSD_HEADS
CHUNK = 128

ATTN_HEADS = 16
ATTN_KV_HEADS = 4
ATTN_GROUP = ATTN_HEADS // ATTN_KV_HEADS
ATTN_HEAD_DIM = 64
Q_DIM = ATTN_HEADS * ATTN_HEAD_DIM
KV_DIM = ATTN_KV_HEADS * ATTN_HEAD_DIM
WINDOW = 128
ATTN_BLOCK = 128
ROPE_BASE = 10000.0

N_EXPERTS = 32
TOP_K = 4
D_FF = D_MODEL
SWIGLU_ALPHA = 1.702
SWIGLU_LIMIT = 7.0

kernel_name = 'hybrid_ssd_swa_moe_diffusion_trunk'


def layer_norm(t, g, b):
    tf = t.astype(jnp.float32)
    mu = jnp.mean(tf, axis=-1, keepdims=True)
    var = jnp.mean(jnp.square(tf - mu), axis=-1, keepdims=True)
    return ((tf - mu) * lax.rsqrt(var + LN_EPS)).astype(t.dtype) * g + b


def centred_depthwise_conv(u, w):
    return lax.conv_general_dilated(
        u, w[:, None, :].astype(u.dtype), window_strides=(1,),
        padding=[(CONV_W // 2, CONV_W // 2)],
        dimension_numbers=('NWC', 'WIO', 'NWC'), feature_group_count=u.shape[-1])


def ssd_scan(x, dt, a, b_in, c_in, init_state):
    bsz, n = x.shape[0], x.shape[1]
    nc = n // CHUNK
    f32 = jnp.float32
    G, HG, P, N = SSD_GROUPS, SSD_HEADS_PER_GROUP, SSD_HEAD_DIM, D_STATE
    xdt = (x.astype(f32) * dt[..., None]).reshape(bsz, nc, CHUNK, G, HG, P)
    bc = b_in.astype(f32).reshape(bsz, nc, CHUNK, G, N)
    cc = c_in.astype(f32).reshape(bsz, nc, CHUNK, G, N)
    to_ghc = lambda t: jnp.transpose(t, (0, 3, 4, 1, 2))
    to_clg = lambda t: jnp.transpose(t, (0, 3, 4, 1, 2))
    cs = jnp.cumsum(to_ghc((dt * a).reshape(bsz, nc, CHUNK, G, HG)), axis=-1)
    tril = jnp.tril(jnp.ones((CHUNK, CHUNK), dtype=bool))
    decay_in = jnp.exp(jnp.where(tril, cs[..., :, None] - cs[..., None, :], -jnp.inf))
    cb = jnp.einsum('bclgn,bcsgn->bgcls', cc, bc)
    y_diag = jnp.einsum('bghcls,bcsghp->bclghp', cb[:, :, None] * decay_in, xdt)
    decay_to_end = to_clg(jnp.exp(cs[..., -1:] - cs))
    chunk_states = jnp.einsum('bclgn,bclghp->bcghpn', bc, xdt * decay_to_end[..., None])
    chunk_decay = jnp.exp(cs[..., -1])

    def step(state, inp):
        dec, st = inp
        return state * dec[..., None, None] + st, state

    final_state, entry_states = lax.scan(
        step, init_state, (jnp.moveaxis(chunk_decay, -1, 0), jnp.moveaxis(chunk_states, 1, 0)))
    y_off = jnp.einsum('bclgn,cbghpn->bclghp', cc, entry_states) * to_clg(jnp.exp(cs))[..., None]
    y = (y_diag + y_off).reshape(bsz, n, G * HG, P)
    return y, final_state


def ssd_mixer(h_lat, h_ctx, w_in, conv_w, conv_b, dt_bias, a_log, d_skip, norm_w, w_out, need_ctx):
    f32 = jnp.float32
    a = -jnp.exp(a_log.astype(f32))
    gn = SSD_GROUPS * D_STATE

    def project(h):
        bsz, n = h.shape[0], h.shape[1]
        p = h @ w_in
        z = p[..., :D_INNER]
        xbc = jax.nn.silu(centred_depthwise_conv(p[..., D_INNER:D_INNER + CONV_DIM], conv_w) + conv_b)
        xs = xbc[..., :D_INNER].reshape(bsz, n, SSD_HEADS, SSD_HEAD_DIM)
        b_in = xbc[..., D_INNER:D_INNER + gn].reshape(bsz, n, SSD_GROUPS, D_STATE)
        c_in = xbc[..., D_INNER + gn:].reshape(bsz, n, SSD_GROUPS, D_STATE)
        dt = jax.nn.softplus(p[..., D_INNER + CONV_DIM:].astype(f32)
                             + dt_bias.reshape(-1).astype(f32)).reshape(bsz, n, 2, SSD_HEADS)
        return z, xs, b_in, c_in, dt

    def bidir(xs, b_in, c_in, dt, init_f, init_b):
        flip = lambda t: jnp.flip(t, axis=1)
        y_f, s_f = ssd_scan(xs, dt[:, :, 0], a[0], b_in, c_in, init_f)
        y_b, s_b = ssd_scan(flip(xs), flip(dt[:, :, 1]), a[1], flip(b_in), flip(c_in), init_b)
        return y_f + flip(y_b), s_f, s_b

    def finish(y, xs, z):
        bsz, n = z.shape[0], z.shape[1]
        y = y + xs.astype(f32) * d_skip.astype(f32)[:, None]
        g = y.reshape(bsz, n, D_INNER) * jax.nn.silu(z.astype(f32))
        g = g.reshape(bsz, n, SSD_GROUPS, D_INNER // SSD_GROUPS)
        g = g * lax.rsqrt(jnp.mean(g * g, axis=-1, keepdims=True) + RMS_EPS)
        g = g.reshape(bsz, n, D_INNER).astype(z.dtype) * norm_w
        return g @ w_out

    z_c, x_c, b_c, c_c, dt_c = project(h_ctx)
    z_l, x_l, b_l, c_l, dt_l = project(h_lat)
    zero = jnp.zeros((h_ctx.shape[0], SSD_GROUPS, SSD_HEADS_PER_GROUP, SSD_HEAD_DIM, D_STATE), f32)
    y_c, s_cf, s_cb = bidir(x_c, b_c, c_c, dt_c, zero, zero)
    y_l, _, _ = bidir(x_l, b_l, c_l, dt_l, s_cf, s_cb)
    y_lat = finish(y_l, x_l, z_l)
    y_ctx = finish(y_c, x_c, z_c) if need_ctx else None
    return y_lat, y_ctx


def axial_rope(n, dtype):
    rows = n // GRID_W
    row = jnp.repeat(jnp.arange(rows), GRID_W).astype(jnp.float32)
    col = jnp.broadcast_to(jnp.arange(GRID_W)[None, :], (rows, GRID_W)).reshape(-1).astype(jnp.float32)
    n_freq = ATTN_HEAD_DIM // 4
    inv = jnp.power(ROPE_BASE, -jnp.arange(n_freq, dtype=jnp.float32) / n_freq)
    ang = jnp.concatenate([row[:, None] * inv, col[:, None] * inv], axis=-1)
    return jnp.cos(ang).astype(dtype), jnp.sin(ang).astype(dtype)


def apply_rope(t, cos, sin):
    half = t.shape[-1] // 2
    t1, t2 = t[..., :half], t[..., half:]
    return jnp.concatenate([t1 * cos - t2 * sin, t2 * cos + t1 * sin], axis=-1)


def band_blocks(t, nb):
    n = t.shape[1]
    tp = jnp.pad(t, ((0, 0), (ATTN_BLOCK, ATTN_BLOCK), (0, 0), (0, 0)))
    parts = [tp[:, o:o + n].reshape(t.shape[0], nb, ATTN_BLOCK, t.shape[2], t.shape[3])
             for o in (0, ATTN_BLOCK, 2 * ATTN_BLOCK)]
    return jnp.concatenate(parts, axis=2)


def band_mask(nb, n):
    qi = jnp.arange(ATTN_BLOCK)[:, None]
    kj = jnp.arange(3 * ATTN_BLOCK)[None, :]
    in_window = jnp.abs(qi + ATTN_BLOCK - kj) <= WINDOW
    key_pos = jnp.arange(nb)[:, None] * ATTN_BLOCK - ATTN_BLOCK + jnp.arange(3 * ATTN_BLOCK)[None, :]
    in_range = (key_pos >= 0) & (key_pos < n)
    return in_window[None] & in_range[:, None, :]


def window_gqa_mixer(h_lat, h_ctx, w_qkv, b_qkv, sinks, w_o, b_o, cos, sin, need_ctx):
    f32 = jnp.float32
    bsz, n_lat = h_lat.shape[0], h_lat.shape[1]
    n_ctx = h_ctx.shape[1]
    scale = ATTN_HEAD_DIM ** -0.5

    def split_qkv(h):
        t = h @ w_qkv + b_qkv
        lead = t.shape[:-1]
        q = t[..., :Q_DIM].reshape(*lead, ATTN_KV_HEADS, ATTN_GROUP, ATTN_HEAD_DIM)
        k = t[..., Q_DIM:Q_DIM + KV_DIM].reshape(*lead, ATTN_KV_HEADS, ATTN_HEAD_DIM)
        v = t[..., Q_DIM + KV_DIM:].reshape(*lead, ATTN_KV_HEADS, ATTN_HEAD_DIM)
        return q, k, v

    q_l, k_l, v_l = split_qkv(h_lat)
    q_c, k_c, v_c = split_qkv(h_ctx)
    q_l = apply_rope(q_l, cos[:, None, None, :], sin[:, None, None, :])
    k_l = apply_rope(k_l, cos[:, None, :], sin[:, None, :])
    sink = sinks.reshape(ATTN_KV_HEADS, ATTN_GROUP).astype(f32)

    nb = n_lat // ATTN_BLOCK
    qb = q_l.reshape(bsz, nb, ATTN_BLOCK, ATTN_KV_HEADS, ATTN_GROUP, ATTN_HEAD_DIM)
    kb = band_blocks(k_l, nb)
    vb = band_blocks(v_l, nb)
    s_band = jnp.einsum('bnqkgd,bnskd->bnkgqs', qb, kb).astype(f32) * scale
    s_band = jnp.where(band_mask(nb, n_lat)[None, :, None, None], s_band, -jnp.inf)
    s_ctx = jnp.einsum('bnqkgd,bckd->bnkgqc', qb, k_c).astype(f32) * scale
    s_sink = jnp.broadcast_to(sink[None, None, :, :, None, None], s_ctx.shape[:-1] + (1,))
    p = jax.nn.softmax(jnp.concatenate([s_ctx, s_band, s_sink], axis=-1), axis=-1).astype(v_l.dtype)
    o_l = (jnp.einsum('bnkgqc,bckd->bnqkgd', p[..., :n_ctx], v_c)
           + jnp.einsum('bnkgqs,bnskd->bnqkgd', p[..., n_ctx:n_ctx + 3 * ATTN_BLOCK], vb))
    y_lat = o_l.reshape(bsz, n_lat, Q_DIM) @ w_o + b_o

    y_ctx = None
    if need_ctx:
        s = jnp.einsum('bqkgd,bckd->bkgqc', q_c, k_c).astype(f32) * scale
        s_sink_c = jnp.broadcast_to(sink[None, :, :, None, None], s.shape[:-1] + (1,))
        pc = jax.nn.softmax(jnp.concatenate([s, s_sink_c], axis=-1), axis=-1).astype(v_c.dtype)
        o_c = jnp.einsum('bkgqc,bckd->bqkgd', pc[..., :n_ctx], v_c)
        y_ctx = o_c.reshape(bsz, n_ctx, Q_DIM) @ w_o + b_o
    return y_lat, y_ctx


def clamped_swiglu(h):
    glu = jnp.minimum(h[..., ::2], SWIGLU_LIMIT)
    lin = jnp.clip(h[..., 1::2], -SWIGLU_LIMIT, SWIGLU_LIMIT)
    return glu * jax.nn.sigmoid(SWIGLU_ALPHA * glu) * (lin + 1.0)


def moe(t, w_r, b_r, w1, b1, w2, b2):
    logits = (t @ w_r + b_r).astype(jnp.float32)
    top_val, top_idx = lax.top_k(logits, TOP_K)
    top_w = jax.nn.softmax(top_val, axis=-1)
    gates = jnp.einsum('tk,tke->te', top_w,
                       jax.nn.one_hot(top_idx, N_EXPERTS, dtype=jnp.float32)).astype(t.dtype)
    out = jnp.zeros_like(t)
    for e in range(N_EXPERTS):
        y_e = clamped_swiglu(t @ w1[e] + b1[e]) @ w2[e] + b2[e]
        out = out + gates[:, e:e + 1] * y_e
    return out


def setup_inputs(seed: int = 0) -> dict:
    key = jax.random.key(seed)
    keys = iter(jax.random.split(key, 32))
    f32 = jnp.float32

    def nrm(shape, scale):
        return scale * jax.random.normal(next(keys), shape, f32)

    x = nrm((BATCH, SEQ, D_MODEL), 1.0)
    c = nrm((BATCH, D_MODEL), 1.0)
    ctx = nrm((BATCH, CTX_LEN, D_MODEL), 1.0)
    c_ctx = nrm((D_MODEL,), 1.0)
    ada_w = nrm((DEPTH, D_MODEL, 6 * D_MODEL), D_MODEL ** -0.5)
    ada_b = nrm((DEPTH, 6 * D_MODEL), 0.02)
    ln_g = 1.0 + nrm((DEPTH, 2, D_MODEL), 0.02)
    ln_b = nrm((DEPTH, 2, D_MODEL), 0.02)
    ssd_w_in = nrm((N_SSD_LAYERS, D_MODEL, SSD_IN_DIM), D_MODEL ** -0.5)
    ssd_conv_w = nrm((N_SSD_LAYERS, CONV_W, CONV_DIM), CONV_W ** -0.5)
    ssd_conv_b = nrm((N_SSD_LAYERS, CONV_DIM), 0.02)
    dt0 = jnp.exp(jax.random.uniform(next(keys), (N_SSD_LAYERS, 2, SSD_HEADS), f32,
                                     minval=math.log(1e-3), maxval=math.log(1e-1)))
    ssd_dt_bias = dt0 + jnp.log(-jnp.expm1(-dt0))
    ssd_a_log = jnp.log(jax.random.uniform(next(keys), (N_SSD_LAYERS, 2, SSD_HEADS), f32,
                                           minval=1.0, maxval=16.0))
    ssd_d = 1.0 + nrm((N_SSD_LAYERS, SSD_HEADS), 0.1)
    ssd_norm_w = 1.0 + nrm((N_SSD_LAYERS, D_INNER), 0.02)
    ssd_w_out = nrm((N_SSD_LAYERS, D_INNER, D_MODEL), BETA * D_INNER ** -0.5)
    attn_w_qkv = nrm((N_ATTN_LAYERS, D_MODEL, Q_DIM + 2 * KV_DIM), D_MODEL ** -0.5)
    attn_b_qkv = nrm((N_ATTN_LAYERS, Q_DIM + 2 * KV_DIM), 0.02)
    attn_sinks = nrm((N_ATTN_LAYERS, ATTN_HEADS), 0.5)
    attn_w_o = nrm((N_ATTN_LAYERS, Q_DIM, D_MODEL), BETA * Q_DIM ** -0.5)
    attn_b_o = nrm((N_ATTN_LAYERS, D_MODEL), 0.02)
    router_w = nrm((DEPTH, D_MODEL, N_EXPERTS), D_MODEL ** -0.5)
    router_b = nrm((DEPTH, N_EXPERTS), 0.01)
    moe_w1 = nrm((DEPTH, N_EXPERTS, D_MODEL, 2 * D_FF), D_MODEL ** -0.5)
    moe_b1 = nrm((DEPTH, N_EXPERTS, 2 * D_FF), 0.02)
    moe_w2 = nrm((DEPTH, N_EXPERTS, D_FF, D_MODEL), BETA * D_FF ** -0.5)
    moe_b2 = nrm((DEPTH, N_EXPERTS, D_MODEL), 0.02)
    return {'x': x, 'c': c, 'ctx': ctx, 'c_ctx': c_ctx, 'ada_w': ada_w, 'ada_b': ada_b,
            'ln_g': ln_g, 'ln_b': ln_b, 'ssd_w_in': ssd_w_in, 'ssd_conv_w': ssd_conv_w,
            'ssd_conv_b': ssd_conv_b, 'ssd_dt_bias': ssd_dt_bias, 'ssd_a_log': ssd_a_log,
            'ssd_d': ssd_d, 'ssd_norm_w': ssd_norm_w, 'ssd_w_out': ssd_w_out,
            'attn_w_qkv': attn_w_qkv, 'attn_b_qkv': attn_b_qkv, 'attn_sinks': attn_sinks,
            'attn_w_o': attn_w_o, 'attn_b_o': attn_b_o, 'router_w': router_w, 'router_b': router_b,
            'moe_w1': moe_w1, 'moe_b1': moe_b1, 'moe_w2': moe_w2, 'moe_b2': moe_b2}


def reference(x, c, ctx, c_ctx, ada_w, ada_b, ln_g, ln_b, ssd_w_in, ssd_conv_w, ssd_conv_b,
              ssd_dt_bias, ssd_a_log, ssd_d, ssd_norm_w, ssd_w_out, attn_w_qkv, attn_b_qkv,
              attn_sinks, attn_w_o, attn_b_o, router_w, router_b, moe_w1, moe_b1, moe_w2, moe_b2):
    bsz, n_lat = x.shape[0], x.shape[1]
    n_lat_tok = bsz * n_lat
    cos, sin = axial_rope(n_lat, x.dtype)
    act_c = jax.nn.silu(c)
    act_cc = jax.nn.silu(c_ctx)
    for i in range(DEPTH):
        need_ctx = i < DEPTH - 1
        m_lat = (act_c @ ada_w[i] + ada_b[i])[:, None, :]
        m_ctx = act_cc @ ada_w[i] + ada_b[i]
        sh1, sc1, g1, sh2, sc2, g2 = jnp.split(m_lat, 6, axis=-1)
        csh1, csc1, cg1, csh2, csc2, cg2 = jnp.split(m_ctx, 6, axis=-1)
        h_lat = x * (1.0 + sc1) + sh1
        h_ctx = ctx * (1.0 + csc1) + csh1
        j = i // N_MIXERS
        if i % N_MIXERS == 0:
            y_lat, y_ctx = ssd_mixer(h_lat, h_ctx, ssd_w_in[j], ssd_conv_w[j], ssd_conv_b[j],
                                     ssd_dt_bias[j], ssd_a_log[j], ssd_d[j], ssd_norm_w[j],
                                     ssd_w_out[j], need_ctx)
        else:
            y_lat, y_ctx = window_gqa_mixer(h_lat, h_ctx, attn_w_qkv[j], attn_b_qkv[j], attn_sinks[j],
                                            attn_w_o[j], attn_b_o[j], cos, sin, need_ctx)
        x = layer_norm(ALPHA * x + g1 * y_lat, ln_g[i, 0], ln_b[i, 0])
        tokens = (x * (1.0 + sc2) + sh2).reshape(n_lat_tok, D_MODEL)
        if need_ctx:
            ctx = layer_norm(ALPHA * ctx + cg1 * y_ctx, ln_g[i, 0], ln_b[i, 0])
            tokens = jnp.concatenate(
                [tokens, (ctx * (1.0 + csc2) + csh2).reshape(-1, D_MODEL)], axis=0)
        f = moe(tokens, router_w[i], router_b[i], moe_w1[i], moe_b1[i], moe_w2[i], moe_b2[i])
        x = layer_norm(ALPHA * x + g2 * f[:n_lat_tok].reshape(x.shape), ln_g[i, 1], ln_b[i, 1])
        if need_ctx:
            ctx = layer_norm(ALPHA * ctx + cg2 * f[n_lat_tok:].reshape(ctx.shape), ln_g[i, 1], ln_b[i, 1])
    return x
```

```python
import functools
import math

import jax
import jax.numpy as jnp
from jax import lax
from jax.experimental import pallas as pl
from jax.experimental.pallas import tpu as pltpu

F32 = jnp.float32
BF16 = jnp.bfloat16
HIGHEST = lax.Precision.HIGHEST

D_MODEL = 1024
DEPTH = 2
GRID_W = 64
ALPHA = (2.0 * DEPTH) ** 0.25
LN_EPS = 1e-5
RMS_EPS = 1e-5

D_INNER = 2 * D_MODEL
SSD_HEAD_DIM = 64
SSD_HEADS = D_INNER // SSD_HEAD_DIM
SSD_GROUPS = 4
SSD_HPG = SSD_HEADS // SSD_GROUPS
D_STATE = 128
GROUP_W = SSD_HPG * SSD_HEAD_DIM
CONV_DIM = D_INNER + 2 * SSD_GROUPS * D_STATE
SSD_IN_DIM = D_INNER + CONV_DIM + 2 * SSD_HEADS
DT_COL = D_INNER + CONV_DIM
CHUNK = 128

ATTN_HEADS = 16
ATTN_KV_HEADS = 4
ATTN_GROUP = ATTN_HEADS // ATTN_KV_HEADS
ATTN_HEAD_DIM = 64
Q_DIM = ATTN_HEADS * ATTN_HEAD_DIM
KV_DIM = ATTN_KV_HEADS * ATTN_HEAD_DIM
QKV_DIM = Q_DIM + 2 * KV_DIM
ATTN_BLOCK = 128
ROPE_BASE = 10000.0

N_EXPERTS = 32
TOP_K = 4
D_FF = D_MODEL
SWIGLU_ALPHA = 1.702
SWIGLU_LIMIT = 7.0

LANES = 128
SUBLANES = 8
ROW_TILE = 256
MOE_TILE = 256
VMEM_LIMIT = 56 * 1024 * 1024


def _cparams(sem):
    return pltpu.CompilerParams(dimension_semantics=sem, vmem_limit_bytes=VMEM_LIMIT)


def _silu(v):
    return v * (1.0 / (1.0 + jnp.exp(-v)))


def _softplus(v):
    return jnp.maximum(v, 0.0) + jnp.log1p(jnp.exp(-jnp.abs(v)))


def _ada_kernel(c_ref, w_ref, b_ref, o_ref):
    a = _silu(c_ref[...])
    o_ref[0] = jnp.dot(a, w_ref[0], precision=HIGHEST, preferred_element_type=F32) + b_ref[0]


def _ada(cond, ada_w, ada_b):
    depth, d, n = ada_w.shape
    tn = 1536
    return pl.pallas_call(
        _ada_kernel,
        grid=(depth, n // tn),
        in_specs=[pl.BlockSpec((SUBLANES, d), lambda l, j: (0, 0)),
                  pl.BlockSpec((1, d, tn), lambda l, j: (l, 0, j)),
                  pl.BlockSpec((1, 1, tn), lambda l, j: (l, 0, j))],
        out_specs=pl.BlockSpec((1, SUBLANES, tn), lambda l, j: (l, 0, j)),
        out_shape=jax.ShapeDtypeStruct((depth, SUBLANES, n), F32),
        compiler_params=_cparams(("parallel", "parallel")),
    )(cond, ada_w, ada_b.reshape(depth, 1, n))


def _inproj_kernel(x_ref, mod_ref, w_ref, o_ref, *, col_chunk):
    m = mod_ref[0, 0]
    h = (x_ref[0] * (1.0 + m[1:2]) + m[0:1]).astype(BF16)
    n = w_ref.shape[1]
    for c0 in range(0, n, col_chunk):
        c1 = min(c0 + col_chunk, n)
        o_ref[0, :, c0:c1] = jnp.dot(h, w_ref[:, c0:c1], preferred_element_type=F32)


def _ssd_inproj(xs, mods, w_bf, n_ctx_t):
    bsz, length, d = xs.shape
    n = w_bf.shape[1]
    return pl.pallas_call(
        functools.partial(_inproj_kernel, col_chunk=1024),
        grid=(bsz, length // ROW_TILE),
        in_specs=[pl.BlockSpec((1, ROW_TILE, d), lambda b, i: (b, i, 0)),
                  pl.BlockSpec((1, 1, 6, d), lambda b, i: (b, jnp.where(i >= n_ctx_t, 1, 0), 0, 0)),
                  pl.BlockSpec((d, n), lambda b, i: (0, 0))],
        out_specs=pl.BlockSpec((1, ROW_TILE, n), lambda b, i: (b, i, 0)),
        out_shape=jax.ShapeDtypeStruct((bsz, length, n), F32),
        compiler_params=_cparams(("parallel", "parallel")),
    )(xs, mods, w_bf)


def _qkv_kernel(x_ref, mod_ref, w_ref, b_ref, cos_ref, sin_ref, o_ref):
    m = mod_ref[0, 0]
    h = (x_ref[0] * (1.0 + m[1:2]) + m[0:1]).astype(BF16)
    cosf = cos_ref[...]
    sinf = sin_ref[...]
    lane = lax.broadcasted_iota(jnp.int32, cosf.shape, 1)
    first_half = (lane % ATTN_HEAD_DIM) < (ATTN_HEAD_DIM // 2)
    scale = ATTN_HEAD_DIM ** -0.5
    n_rope = (Q_DIM + KV_DIM) // LANES
    for c in range(QKV_DIM // LANES):
        t = jnp.dot(h, w_ref[:, c * LANES:(c + 1) * LANES], preferred_element_type=F32)
        t = t + b_ref[:, c * LANES:(c + 1) * LANES]
        if c < n_rope:
            swapped = jnp.where(first_half,
                                pltpu.roll(t, LANES - ATTN_HEAD_DIM // 2, axis=1),
                                pltpu.roll(t, ATTN_HEAD_DIM // 2, axis=1))
            t = t * cosf + swapped * sinf
        if c < Q_DIM // LANES:
            t = t * scale
        o_ref[0, :, c * LANES:(c + 1) * LANES] = t.astype(o_ref.dtype)


def _attn_qkv(xs, mods, w_bf, b, cos_t, sin_t, n_ctx_t):
    bsz, length, d = xs.shape
    n = w_bf.shape[1]
    return pl.pallas_call(
        _qkv_kernel,
        grid=(bsz, length // ROW_TILE),
        in_specs=[pl.BlockSpec((1, ROW_TILE, d), lambda b_, i: (b_, i, 0)),
                  pl.BlockSpec((1, 1, 6, d), lambda b_, i: (b_, jnp.where(i >= n_ctx_t, 1, 0), 0, 0)),
                  pl.BlockSpec((d, n), lambda b_, i: (0, 0)),
                  pl.BlockSpec((1, n), lambda b_, i: (0, 0)),
                  pl.BlockSpec((ROW_TILE, LANES), lambda b_, i: (i, 0)),
                  pl.BlockSpec((ROW_TILE, LANES), lambda b_, i: (i, 0))],
        out_specs=pl.BlockSpec((1, ROW_TILE, n), lambda b_, i: (b_, i, 0)),
        out_shape=jax.ShapeDtypeStruct((bsz, length, n), BF16),
        compiler_params=_cparams(("parallel", "parallel")),
    )(xs, mods, w_bf, b.reshape(1, n), cos_t, sin_t)


def _conv_kernel(u_ref, prev_ref, next_ref, w_ref, b_ref, o_ref, *, n_ctx_t, n_t):
    i = pl.program_id(1)
    u = u_ref[0]
    rows = u.shape[0]
    starts_segment = jnp.logical_or(i == 0, i == n_ctx_t)
    ends_segment = jnp.logical_or(i == n_ctx_t - 1, i == n_t - 1)
    before = jnp.where(starts_segment, 0.0, prev_ref[0, SUBLANES - 1:SUBLANES, :])
    after = jnp.where(ends_segment, 0.0, next_ref[0, 0:1, :])
    row = lax.broadcasted_iota(jnp.int32, u.shape, 0)
    up = jnp.where(row == 0, before, pltpu.roll(u, 1, axis=0))
    dn = jnp.where(row == rows - 1, after, pltpu.roll(u, rows - 1, axis=0))
    w = w_ref[...]
    v = up * w[0:1] + u * w[1:2] + dn * w[2:3] + b_ref[...]
    o_ref[0] = _silu(v)


def _ssd_conv(p, conv_w, conv_b, n_ctx_t):
    bsz, length, _ = p.shape
    n_t = length // ROW_TILE
    cw = 1024
    col0 = D_INNER // cw
    r8 = ROW_TILE // SUBLANES
    last8 = length // SUBLANES - 1
    return pl.pallas_call(
        functools.partial(_conv_kernel, n_ctx_t=n_ctx_t, n_t=n_t),
        grid=(bsz, n_t, CONV_DIM // cw),
        in_specs=[pl.BlockSpec((1, ROW_TILE, cw), lambda b, i, j: (b, i, col0 + j)),
                  pl.BlockSpec((1, SUBLANES, cw), lambda b, i, j: (b, jnp.maximum(i * r8 - 1, 0), col0 + j)),
                  pl.BlockSpec((1, SUBLANES, cw), lambda b, i, j: (b, jnp.minimum((i + 1) * r8, last8), col0 + j)),
                  pl.BlockSpec((3, cw), lambda b, i, j: (0, j)),
                  pl.BlockSpec((1, cw), lambda b, i, j: (0, j))],
        out_specs=pl.BlockSpec((1, ROW_TILE, cw), lambda b, i, j: (b, i, j)),
        out_shape=jax.ShapeDtypeStruct((bsz, length, CONV_DIM), F32),
        compiler_params=_cparams(("parallel", "parallel", "parallel")),
    )(p, p, p, conv_w, conv_b.reshape(1, CONV_DIM))


def _scan_kernel(x_ref, b_ref, c_ref, dt_ref, bias_ref, alog_ref, y_ref, state_ref):
    d = pl.program_id(1)
    c = pl.program_id(2)
    fwd = d == 0

    @pl.when(c == 0)
    def _():
        state_ref[...] = jnp.zeros_like(state_ref)

    h2 = 2 * SSD_HEADS
    dt_all = _softplus(dt_ref[0][:, :h2] + bias_ref[...])
    a_all = -jnp.exp(alog_ref[...])
    dt = jnp.where(fwd, dt_all[:, :SSD_HEADS], dt_all[:, SSD_HEADS:])
    a = jnp.where(fwd, a_all[:, :SSD_HEADS], a_all[:, SSD_HEADS:])
    da = dt * a

    li = lax.broadcasted_iota(jnp.int32, (CHUNK, CHUNK), 0)
    si = lax.broadcasted_iota(jnp.int32, (CHUNK, CHUNK), 1)
    tri = jnp.where(fwd, si - li, li - si) <= 0
    tri_f = tri.astype(F32)
    cs = jnp.dot(tri_f, da, precision=HIGHEST, preferred_element_type=F32)
    cs_t = lax.dot_general(da, tri_f, (((0,), (1,)), ((), ())), precision=HIGHEST,
                           preferred_element_type=F32)
    tot = jnp.sum(da, axis=0, keepdims=True)

    fac = jnp.concatenate([dt, jnp.exp(tot - cs), jnp.exp(cs),
                           jnp.broadcast_to(jnp.exp(tot), (SUBLANES, SSD_HEADS))], axis=0)
    hi = fac.astype(BF16)
    lo = (fac - hi.astype(F32)).astype(BF16)
    hh = lax.broadcasted_iota(jnp.int32, (SSD_HEADS, D_INNER), 0)
    ch = lax.broadcasted_iota(jnp.int32, (SSD_HEADS, D_INNER), 1)
    expand = jnp.where(ch // SSD_HEAD_DIM == hh, 1.0, 0.0).astype(BF16)
    wide = (jnp.dot(hi, expand, preferred_element_type=F32)
            + jnp.dot(lo, expand, preferred_element_type=F32))
    dt_w = wide[0:CHUNK]
    to_end_w = wide[CHUNK:2 * CHUNK]
    from_start_w = wide[2 * CHUNK:3 * CHUNK]
    chunk_decay_w = wide[3 * CHUNK:3 * CHUNK + 1]

    xdt = x_ref[0] * dt_w
    xdt_b = xdt.astype(BF16)
    xend_b = (xdt * to_end_w).astype(BF16)

    for g in range(SSD_GROUPS):
        gs = slice(g * GROUP_W, (g + 1) * GROUP_W)
        bg = b_ref[0][:, g * D_STATE:(g + 1) * D_STATE].astype(BF16)
        cg = c_ref[0][:, g * D_STATE:(g + 1) * D_STATE].astype(BF16)
        cb = lax.dot_general(cg, bg, (((1,), (1,)), ((), ())), preferred_element_type=F32)
        st = state_ref[g]
        y_off = jnp.dot(cg, st.astype(BF16), preferred_element_type=F32) * from_start_w[:, gs]
        for pair in range(SSD_HPG // 2):
            parts = []
            for k in range(2):
                h = g * SSD_HPG + pair * 2 + k
                diff = cs[:, h:h + 1] - cs_t[h:h + 1, :]
                decay = jnp.exp(jnp.where(tri, diff, -jnp.inf))
                lmat = (cb * decay).astype(BF16)
                parts.append(jnp.dot(lmat, xdt_b[:, h * SSD_HEAD_DIM:(h + 1) * SSD_HEAD_DIM],
                                     preferred_element_type=F32))
            c0 = pair * 2 * SSD_HEAD_DIM
            y_ref[0, 0, :, g * GROUP_W + c0:g * GROUP_W + c0 + 2 * SSD_HEAD_DIM] = (
                jnp.concatenate(parts, axis=1) + y_off[:, c0:c0 + 2 * SSD_HEAD_DIM])
        new = lax.dot_general(bg, xend_b[:, gs], (((0,), (0,)), ((), ())), preferred_element_type=F32)
        state_ref[g] = st * chunk_decay_w[:, gs] + new


def _ssd_scan(xbc, p, dt_bias, a_log, n_ctx_c):
    bsz, length, _ = xbc.shape
    nc = length // CHUNK
    h2 = 2 * SSD_HEADS

    def chunk_of(d, c):
        back = jnp.where(c < n_ctx_c, n_ctx_c - 1 - c, nc - 1 - (c - n_ctx_c))
        return jnp.where(d == 0, c, back)

    gn = SSD_GROUPS * D_STATE
    return pl.pallas_call(
        _scan_kernel,
        grid=(bsz, 2, nc),
        in_specs=[pl.BlockSpec((1, CHUNK, D_INNER), lambda b, d, c: (b, chunk_of(d, c), 0)),
                  pl.BlockSpec((1, CHUNK, gn), lambda b, d, c: (b, chunk_of(d, c), D_INNER // gn)),
                  pl.BlockSpec((1, CHUNK, gn), lambda b, d, c: (b, chunk_of(d, c), D_INNER // gn + 1)),
                  pl.BlockSpec((1, CHUNK, LANES), lambda b, d, c: (b, chunk_of(d, c), DT_COL // LANES)),
                  pl.BlockSpec((1, h2), lambda b, d, c: (0, 0)),
                  pl.BlockSpec((1, h2), lambda b, d, c: (0, 0))],
        out_specs=pl.BlockSpec((1, 1, CHUNK, D_INNER), lambda b, d, c: (d, b, chunk_of(d, c), 0)),
        out_shape=jax.ShapeDtypeStruct((2, bsz, length, D_INNER), F32),
        scratch_shapes=[pltpu.VMEM((SSD_GROUPS, D_STATE, GROUP_W), F32)],
        compiler_params=_cparams(("parallel", "arbitrary", "arbitrary")),
    )(xbc, xbc, xbc, p, dt_bias.reshape(1, h2), a_log.reshape(1, h2))


def _finish_kernel(yf_ref, yb_ref, xs_ref, z_ref, dskip_ref, nw_ref, o_ref):
    y = yf_ref[0, 0] + yb_ref[0, 0] + xs_ref[0] * dskip_ref[...]
    g = y * _silu(z_ref[0])
    for k in range(SSD_GROUPS):
        gs = slice(k * GROUP_W, (k + 1) * GROUP_W)
        gk = g[:, gs]
        ms = jnp.mean(gk * gk, axis=-1, keepdims=True)
        o_ref[0, :, gs] = ((gk * lax.rsqrt(ms + RMS_EPS)) * nw_ref[:, gs]).astype(o_ref.dtype)


def _ssd_finish(y2, xbc, p, d_wide, norm_w):
    _, bsz, length, _ = y2.shape
    return pl.pallas_call(
        _finish_kernel,
        grid=(bsz, length // ROW_TILE),
        in_specs=[pl.BlockSpec((1, 1, ROW_TILE, D_INNER), lambda b, i: (0, b, i, 0)),
                  pl.BlockSpec((1, 1, ROW_TILE, D_INNER), lambda b, i: (1, b, i, 0)),
                  pl.BlockSpec((1, ROW_TILE, D_INNER), lambda b, i: (b, i, 0)),
                  pl.BlockSpec((1, ROW_TILE, D_INNER), lambda b, i: (b, i, 0)),
                  pl.BlockSpec((1, D_INNER), lambda b, i: (0, 0)),
                  pl.BlockSpec((1, D_INNER), lambda b, i: (0, 0))],
        out_specs=pl.BlockSpec((1, ROW_TILE, D_INNER), lambda b, i: (b, i, 0)),
        out_shape=jax.ShapeDtypeStruct((bsz, length, D_INNER), BF16),
        compiler_params=_cparams(("parallel", "parallel")),
    )(y2, y2, xbc, p, d_wide, norm_w.reshape(1, D_INNER))


def _attn_kernel(q_ref, kc_ref, vc_ref, kp_ref, ko_ref, kn_ref, vp_ref, vo_ref, vn_ref,
                 sink_ref, o_ref, *, n_blocks):
    qb = pl.program_id(1)
    n_ctx = kc_ref.shape[1]
    blk = ATTN_BLOCK
    hd = ATTN_HEAD_DIM
    rows = ATTN_GROUP * blk

    n_keys = n_ctx + 3 * blk
    qi = lax.broadcasted_iota(jnp.int32, (rows, n_keys), 0) % blk
    kj = lax.broadcasted_iota(jnp.int32, (rows, n_keys), 1) - n_ctx
    rel = kj - blk - qi
    in_window = jnp.logical_and(rel >= -blk, rel <= blk)
    in_range = jnp.logical_and(jnp.logical_or(kj >= blk, qb > 0),
                               jnp.logical_or(kj < 2 * blk, qb < n_blocks - 1))
    valid = jnp.logical_or(kj < 0, jnp.logical_and(in_window, in_range))

    q = q_ref[0]
    for k in range(ATTN_KV_HEADS):
        ks = slice(k * hd, (k + 1) * hd)
        keys = jnp.concatenate([kc_ref[0][:, ks], kp_ref[0][:, ks], ko_ref[0][:, ks], kn_ref[0][:, ks]], axis=0)
        vals = jnp.concatenate([vc_ref[0][:, ks], vp_ref[0][:, ks], vo_ref[0][:, ks], vn_ref[0][:, ks]], axis=0)
        qs = jnp.concatenate([q[:, (k * ATTN_GROUP + g) * hd:(k * ATTN_GROUP + g + 1) * hd]
                              for g in range(ATTN_GROUP)], axis=0)
        s = lax.dot_general(qs, keys, (((1,), (1,)), ((), ())), preferred_element_type=F32)
        s = jnp.where(valid, s, -jnp.inf)
        sink = sink_ref[k * rows:(k + 1) * rows, 0:1]
        m = jnp.maximum(jnp.max(s, axis=-1, keepdims=True), sink)
        e = jnp.exp(s - m)
        denom = jnp.sum(e, axis=-1, keepdims=True) + jnp.exp(sink - m)
        pr = (e / denom).astype(BF16)
        o = jnp.dot(pr, vals, preferred_element_type=F32)
        for pair in range(ATTN_GROUP // 2):
            g0 = 2 * pair
            both = jnp.concatenate([o[g0 * blk:(g0 + 1) * blk], o[(g0 + 1) * blk:(g0 + 2) * blk]], axis=1)
            c0 = (k * ATTN_GROUP + g0) * hd
            o_ref[0, :, c0:c0 + 2 * hd] = both.astype(o_ref.dtype)


def _attention(qkv, sink_rows, n_ctx):
    bsz, length, _ = qkv.shape
    n_lat = length - n_ctx
    nb = n_lat // ATTN_BLOCK
    cb = n_ctx // ATTN_BLOCK
    kcol = Q_DIM // KV_DIM
    vcol = kcol + 1

    def band(col, off):
        return pl.BlockSpec(
            (1, ATTN_BLOCK, KV_DIM),
            lambda b, i: (b, cb + jnp.clip(i + off, 0, nb - 1), col))

    return pl.pallas_call(
        functools.partial(_attn_kernel, n_blocks=nb),
        grid=(bsz, nb),
        in_specs=[pl.BlockSpec((1, ATTN_BLOCK, Q_DIM), lambda b, i: (b, cb + i, 0)),
                  pl.BlockSpec((1, n_ctx, KV_DIM), lambda b, i: (b, 0, kcol)),
                  pl.BlockSpec((1, n_ctx, KV_DIM), lambda b, i: (b, 0, vcol)),
                  band(kcol, -1), band(kcol, 0), band(kcol, 1),
                  band(vcol, -1), band(vcol, 0), band(vcol, 1),
                  pl.BlockSpec(sink_rows.shape, lambda b, i: (0, 0))],
        out_specs=pl.BlockSpec((1, ATTN_BLOCK, Q_DIM), lambda b, i: (b, i, 0)),
        out_shape=jax.ShapeDtypeStruct((bsz, n_lat, Q_DIM), BF16),
        compiler_params=_cparams(("parallel", "parallel")),
    )(qkv, qkv, qkv, qkv, qkv, qkv, qkv, qkv, qkv, sink_rows)


def _layer_norm(v, g, b):
    mu = jnp.mean(v, axis=-1, keepdims=True)
    cen = v - mu
    var = jnp.mean(cen * cen, axis=-1, keepdims=True)
    return cen * lax.rsqrt(var + LN_EPS) * g + b


def _proj_ln_kernel(a_ref, w_ref, bo_ref, x_ref, mod_ref, lng_ref, lnb_ref, wr_ref, br_ref,
                    x1_ref, tok_ref, ti_ref, tw_ref):
    m = mod_ref[0, 0]
    y = jnp.dot(a_ref[0], w_ref[...], preferred_element_type=F32) + bo_ref[...]
    x1 = _layer_norm(ALPHA * x_ref[0] + m[2:3] * y, lng_ref[...], lnb_ref[...])
    x1_ref[0] = x1
    tok = x1 * (1.0 + m[4:5]) + m[3:4]
    tok_ref[0] = tok.astype(tok_ref.dtype)
    logits = jnp.dot(tok, wr_ref[...], precision=HIGHEST, preferred_element_type=F32) + br_ref[...]
    lane = lax.broadcasted_iota(jnp.int32, logits.shape, 1)
    vals = logits
    top_v, top_i = [], []
    for _ in range(TOP_K):
        mx = jnp.max(vals, axis=-1, keepdims=True)
        idx = jnp.min(jnp.where(vals == mx, lane, N_EXPERTS), axis=-1, keepdims=True)
        vals = jnp.where(lane == idx, -jnp.inf, vals)
        top_v.append(mx)
        top_i.append(idx)
    ex = [jnp.exp(v - top_v[0]) for v in top_v]
    den = ex[0] + ex[1] + ex[2] + ex[3]
    ti_ref[0] = jnp.concatenate(top_i, axis=1)
    tw_ref[0] = jnp.concatenate([e / den for e in ex], axis=1)


def _proj_ln(a, w_bf, b_o, xs, mods, ln_g, ln_b, w_r, b_r, *, row_off_t, n_ctx_t):
    bsz, rows, kdim = a.shape
    d = xs.shape[2]

    def mod_idx(b, i):
        return (b, jnp.where(i + row_off_t >= n_ctx_t, 1, 0), 0, 0)

    row_spec = lambda width: pl.BlockSpec((1, ROW_TILE, width), lambda b, i: (b, i, 0))
    vec_spec = pl.BlockSpec((1, d), lambda b, i: (0, 0))
    return pl.pallas_call(
        _proj_ln_kernel,
        grid=(bsz, rows // ROW_TILE),
        in_specs=[row_spec(kdim),
                  pl.BlockSpec((kdim, d), lambda b, i: (0, 0)),
                  vec_spec,
                  pl.BlockSpec((1, ROW_TILE, d), lambda b, i: (b, i + row_off_t, 0)),
                  pl.BlockSpec((1, 1, 6, d), mod_idx),
                  vec_spec, vec_spec,
                  pl.BlockSpec((d, N_EXPERTS), lambda b, i: (0, 0)),
                  pl.BlockSpec((1, N_EXPERTS), lambda b, i: (0, 0))],
        out_specs=[row_spec(d), row_spec(d), row_spec(TOP_K), row_spec(TOP_K)],
        out_shape=[jax.ShapeDtypeStruct((bsz, rows, d), F32),
                   jax.ShapeDtypeStruct((bsz, rows, d), BF16),
                   jax.ShapeDtypeStruct((bsz, rows, TOP_K), jnp.int32),
                   jax.ShapeDtypeStruct((bsz, rows, TOP_K), F32)],
        compiler_params=_cparams(("parallel", "parallel")),
    )(a, w_bf, b_o.reshape(1, d), xs, mods, ln_g.reshape(1, d), ln_b.reshape(1, d),
      w_r, b_r.reshape(1, N_EXPERTS))


def _expert_kernel(te_ref, tv_ref, x_ref, w1g_ref, w1l_ref, b1g_ref, b1l_ref, w2_ref, b2_ref, y_ref):
    i = pl.program_id(0)

    @pl.when(tv_ref[i] > 0)
    def _():
        x = x_ref[...]
        glu = jnp.dot(x, w1g_ref[0], preferred_element_type=F32) + b1g_ref[0]
        lin = jnp.dot(x, w1l_ref[0], preferred_element_type=F32) + b1l_ref[0]
        glu = jnp.minimum(glu, SWIGLU_LIMIT)
        lin = jnp.clip(lin, -SWIGLU_LIMIT, SWIGLU_LIMIT)
        act = glu * (1.0 / (1.0 + jnp.exp(-SWIGLU_ALPHA * glu))) * (lin + 1.0)
        y_ref[...] = jnp.dot(act.astype(BF16), w2_ref[0], preferred_element_type=F32) + b2_ref[0]

    @pl.when(tv_ref[i] == 0)
    def _():
        y_ref[...] = jnp.zeros_like(y_ref)


def _experts(tile_e, tile_v, xg, w1g, w1l, b1g, b1l, w2, b2):
    ns, d = xg.shape
    f = w1g.shape[2]
    nt = ns // MOE_TILE
    grid_spec = pltpu.PrefetchScalarGridSpec(
        num_scalar_prefetch=2,
        grid=(nt,),
        in_specs=[pl.BlockSpec((MOE_TILE, d), lambda i, te, tv: (i, 0)),
                  pl.BlockSpec((1, d, f), lambda i, te, tv: (te[i], 0, 0)),
                  pl.BlockSpec((1, d, f), lambda i, te, tv: (te[i], 0, 0)),
                  pl.BlockSpec((1, 1, f), lambda i, te, tv: (te[i], 0, 0)),
                  pl.BlockSpec((1, 1, f), lambda i, te, tv: (te[i], 0, 0)),
                  pl.BlockSpec((1, f, d), lambda i, te, tv: (te[i], 0, 0)),
                  pl.BlockSpec((1, 1, d), lambda i, te, tv: (te[i], 0, 0))],
        out_specs=pl.BlockSpec((MOE_TILE, d), lambda i, te, tv: (i, 0)),
    )
    return pl.pallas_call(
        _expert_kernel,
        grid_spec=grid_spec,
        out_shape=jax.ShapeDtypeStruct((ns, d), F32),
        compiler_params=_cparams(("arbitrary",)),
    )(tile_e, tile_v, xg, w1g, w1l, b1g, b1l, w2, b2)


def _combine_kernel(yg_ref, tw_ref, x_ref, mod_ref, lng_ref, lnb_ref, o_ref):
    m = mod_ref[0, 0]
    d = x_ref.shape[2]
    tw = tw_ref[0]
    f = tw[:, 0:1] * yg_ref[0, :, 0:d]
    for k in range(1, TOP_K):
        f = f + tw[:, k:k + 1] * yg_ref[0, :, k * d:(k + 1) * d]
    o_ref[0] = _layer_norm(ALPHA * x_ref[0] + m[5:6] * f, lng_ref[...], lnb_ref[...])


def _combine(yg, tw, x1, mods, ln_g, ln_b, *, row_off_t, n_ctx_t):
    bsz, rows, d = x1.shape

    def mod_idx(b, i):
        return (b, jnp.where(i + row_off_t >= n_ctx_t, 1, 0), 0, 0)

    vec_spec = pl.BlockSpec((1, d), lambda b, i: (0, 0))
    return pl.pallas_call(
        _combine_kernel,
        grid=(bsz, rows // ROW_TILE),
        in_specs=[pl.BlockSpec((1, ROW_TILE, TOP_K * d), lambda b, i: (b, i, 0)),
                  pl.BlockSpec((1, ROW_TILE, TOP_K), lambda b, i: (b, i, 0)),
                  pl.BlockSpec((1, ROW_TILE, d), lambda b, i: (b, i, 0)),
                  pl.BlockSpec((1, 1, 6, d), mod_idx),
                  vec_spec, vec_spec],
        out_specs=pl.BlockSpec((1, ROW_TILE, d), lambda b, i: (b, i, 0)),
        out_shape=jax.ShapeDtypeStruct((bsz, rows, d), F32),
        compiler_params=_cparams(("parallel", "parallel")),
    )(yg, tw, x1, mods, ln_g.reshape(1, d), ln_b.reshape(1, d))


def _moe(tok, top_i, top_w, w1, b1, w2, b2):
    bsz, rows, d = tok.shape
    t = bsz * rows
    n_assign = t * TOP_K
    e_flat = top_i.reshape(n_assign)
    order = jnp.argsort(e_flat, stable=True)
    e_sorted = e_flat[order]
    counts = jnp.zeros((N_EXPERTS,), jnp.int32).at[e_flat].add(1)
    padded = ((counts + MOE_TILE - 1) // MOE_TILE) * MOE_TILE
    pad_end = jnp.cumsum(padded)
    pad_off = pad_end - padded
    off = jnp.cumsum(counts) - counts
    slot_sorted = pad_off[e_sorted] + (jnp.arange(n_assign, dtype=jnp.int32) - off[e_sorted])
    n_tiles = n_assign // MOE_TILE + N_EXPERTS
    n_slots = n_tiles * MOE_TILE
    src_tok = jnp.zeros((n_slots,), jnp.int32).at[slot_sorted].set((order // TOP_K).astype(jnp.int32))
    slot_of = jnp.zeros((n_assign,), jnp.int32).at[order].set(slot_sorted)
    tile_start = jnp.arange(n_tiles, dtype=jnp.int32) * MOE_TILE
    tile_e = jnp.searchsorted(pad_end, tile_start, side="right").astype(jnp.int32)
    tile_v = (tile_e < N_EXPERTS).astype(jnp.int32)
    tile_e = jnp.minimum(tile_e, N_EXPERTS - 1)

    xg = jnp.take(tok.reshape(t, d), src_tok, axis=0)
    f = w1.shape[2] // 2
    w1g = w1[:, :, 0::2].astype(BF16)
    w1l = w1[:, :, 1::2].astype(BF16)
    b1g = b1[:, 0::2].reshape(N_EXPERTS, 1, f)
    b1l = b1[:, 1::2].reshape(N_EXPERTS, 1, f)
    y = _experts(tile_e, tile_v, xg, w1g, w1l, b1g, b1l, w2.astype(BF16), b2.reshape(N_EXPERTS, 1, d))
    return jnp.take(y, slot_of, axis=0).reshape(bsz, rows, TOP_K * d)


def _rope_tables(n_lat, n_ctx):
    rows = n_lat // GRID_W
    row = jnp.repeat(jnp.arange(rows), GRID_W).astype(F32)
    col = jnp.broadcast_to(jnp.arange(GRID_W)[None, :], (rows, GRID_W)).reshape(-1).astype(F32)
    n_freq = ATTN_HEAD_DIM // 4
    inv = jnp.power(ROPE_BASE, -jnp.arange(n_freq, dtype=F32) / n_freq)
    ang = jnp.concatenate([row[:, None] * inv, col[:, None] * inv], axis=-1)
    cos, sin = jnp.cos(ang), jnp.sin(ang)
    cos_h = jnp.concatenate([cos, cos], axis=-1)
    sin_h = jnp.concatenate([-sin, sin], axis=-1)
    reps = LANES // ATTN_HEAD_DIM
    cos_t = jnp.concatenate([jnp.ones((n_ctx, LANES), F32), jnp.tile(cos_h, (1, reps))], axis=0)
    sin_t = jnp.concatenate([jnp.zeros((n_ctx, LANES), F32), jnp.tile(sin_h, (1, reps))], axis=0)
    return cos_t, sin_t


def kernel(x, c, ctx, c_ctx, ada_w, ada_b, ln_g, ln_b, ssd_w_in, ssd_conv_w, ssd_conv_b, ssd_dt_bias,
           ssd_a_log, ssd_d, ssd_norm_w, ssd_w_out, attn_w_qkv, attn_b_qkv, attn_sinks, attn_w_o, attn_b_o,
           router_w, router_b, moe_w1, moe_b1, moe_w2, moe_b2):
    bsz, n_lat, d = x.shape
    n_ctx = ctx.shape[1]
    assert d == D_MODEL and bsz + 1 <= SUBLANES
    assert n_ctx % ROW_TILE == 0 and n_lat % ROW_TILE == 0 and n_lat % GRID_W == 0
    n_ctx_t = n_ctx // ROW_TILE

    cond = jnp.concatenate([c, c_ctx[None], jnp.zeros((SUBLANES - bsz - 1, d), F32)], axis=0)
    ada = _ada(cond, ada_w, ada_b)
    xs = jnp.concatenate([ctx, x], axis=1)

    for i in range(DEPTH):
        last = i == DEPTH - 1
        m_lat = ada[i, :bsz].reshape(bsz, 1, 6, d)
        m_ctx = jnp.broadcast_to(ada[i, bsz].reshape(1, 1, 6, d), (bsz, 1, 6, d))
        mods = jnp.concatenate([m_ctx, m_lat], axis=1)
        j = i // 2
        if i % 2 == 0:
            p = _ssd_inproj(xs, mods, ssd_w_in[j].astype(BF16), n_ctx_t)
            xbc = _ssd_conv(p, ssd_conv_w[j], ssd_conv_b[j], n_ctx_t)
            y2 = _ssd_scan(xbc, p, ssd_dt_bias[j], ssd_a_log[j], n_ctx // CHUNK)
            d_wide = jnp.repeat(ssd_d[j], SSD_HEAD_DIM).reshape(1, D_INNER)
            mix = _ssd_finish(y2, xbc, p, d_wide, ssd_norm_w[j])
            w_o, b_o = ssd_w_out[j].astype(BF16), jnp.zeros((d,), F32)
            if last:
                mix = mix[:, n_ctx:]
        else:
            cos_t, sin_t = _rope_tables(n_lat, n_ctx)
            qkv = _attn_qkv(xs, mods, attn_w_qkv[j].astype(BF16), attn_b_qkv[j], cos_t, sin_t, n_ctx_t)
            sink_rows = jnp.broadcast_to(
                jnp.repeat(attn_sinks[j].astype(F32), ATTN_BLOCK)[:, None], (ATTN_HEADS * ATTN_BLOCK, LANES))
            mix = _attention(qkv, sink_rows, n_ctx)
            w_o, b_o = attn_w_o[j].astype(BF16), attn_b_o[j]
            if not last:
                raise NotImplementedError("context queries are only needed when an attention layer is not last")
        row_off_t = n_ctx_t if last else 0
        x1, tok, top_i, top_w = _proj_ln(mix, w_o, b_o, xs, mods, ln_g[i, 0], ln_b[i, 0],
                                         router_w[i], router_b[i], row_off_t=row_off_t, n_ctx_t=n_ctx_t)
        yg = _moe(tok, top_i, top_w, moe_w1[i], moe_b1[i], moe_w2[i], moe_b2[i])
        xs = _combine(yg, top_w, x1, mods, ln_g[i, 1], ln_b[i, 1], row_off_t=row_off_t, n_ctx_t=n_ctx_t)
    return xs
```

```python
import functools

import jax
import jax.numpy as jnp
from jax import lax
from jax.experimental import pallas as pl
from jax.experimental.pallas import tpu as pltpu

F32 = jnp.float32
BF16 = jnp.bfloat16
HIGHEST = lax.Precision.HIGHEST

D_MODEL = 1024
DEPTH = 2
GRID_W = 64
ALPHA = (2.0 * DEPTH) ** 0.25
LN_EPS = 1e-5
RMS_EPS = 1e-5

D_INNER = 2 * D_MODEL
SSD_HEAD_DIM = 64
SSD_HEADS = D_INNER // SSD_HEAD_DIM
SSD_GROUPS = 4
SSD_HPG = SSD_HEADS // SSD_GROUPS
D_STATE = 128
GROUP_W = SSD_HPG * SSD_HEAD_DIM
CONV_DIM = D_INNER + 2 * SSD_GROUPS * D_STATE
SSD_IN_DIM = D_INNER + CONV_DIM + 2 * SSD_HEADS
DT_COL = D_INNER + CONV_DIM
CHUNK = 128

ATTN_HEADS = 16
ATTN_KV_HEADS = 4
ATTN_GROUP = ATTN_HEADS // ATTN_KV_HEADS
ATTN_HEAD_DIM = 64
Q_DIM = ATTN_HEADS * ATTN_HEAD_DIM
KV_DIM = ATTN_KV_HEADS * ATTN_HEAD_DIM
QKV_DIM = Q_DIM + 2 * KV_DIM
ATTN_BLOCK = 128
ROPE_BASE = 10000.0

N_EXPERTS = 32
TOP_K = 4
D_FF = D_MODEL
SWIGLU_ALPHA = 1.702
SWIGLU_LIMIT = 7.0

LANES = 128
SUBLANES = 8
ROW_TILE = 256
MOE_TILE = 256
VMEM_LIMIT = 56 * 1024 * 1024


def _cparams(sem):
    return pltpu.CompilerParams(dimension_semantics=sem, vmem_limit_bytes=VMEM_LIMIT)


def _silu(v):
    return v * (1.0 / (1.0 + jnp.exp(-v)))


def _softplus(v):
    return jnp.maximum(v, 0.0) + jnp.log1p(jnp.exp(-jnp.abs(v)))


def _ada_kernel(c_ref, w_ref, b_ref, o_ref):
    a = _silu(c_ref[...])
    o_ref[0] = jnp.dot(a, w_ref[0], precision=HIGHEST, preferred_element_type=F32) + b_ref[0]


def _ada(cond, ada_w, ada_b):
    depth, d, n = ada_w.shape
    tn = 1536
    return pl.pallas_call(
        _ada_kernel,
        grid=(depth, n // tn),
        in_specs=[pl.BlockSpec((SUBLANES, d), lambda l, j: (0, 0)),
                  pl.BlockSpec((1, d, tn), lambda l, j: (l, 0, j)),
                  pl.BlockSpec((1, 1, tn), lambda l, j: (l, 0, j))],
        out_specs=pl.BlockSpec((1, SUBLANES, tn), lambda l, j: (l, 0, j)),
        out_shape=jax.ShapeDtypeStruct((depth, SUBLANES, n), F32),
        compiler_params=_cparams(("parallel", "parallel")),
    )(cond, ada_w, ada_b.reshape(depth, 1, n))


def _inproj_kernel(x_ref, mod_ref, w_ref, o_ref, *, col_chunk):
    m = mod_ref[0, 0]
    h = (x_ref[0] * (1.0 + m[1:2]) + m[0:1]).astype(BF16)
    n = w_ref.shape[1]
    for c0 in range(0, n, col_chunk):
        c1 = min(c0 + col_chunk, n)
        o_ref[0, :, c0:c1] = jnp.dot(h, w_ref[:, c0:c1], preferred_element_type=F32)


def _ssd_inproj(xs, mods, w_bf, n_ctx_t):
    bsz, length, d = xs.shape
    n = w_bf.shape[1]
    return pl.pallas_call(
        functools.partial(_inproj_kernel, col_chunk=1024),
        grid=(bsz, length // ROW_TILE),
        in_specs=[pl.BlockSpec((1, ROW_TILE, d), lambda b, i: (b, i, 0)),
                  pl.BlockSpec((1, 1, 6, d), lambda b, i: (b, jnp.where(i >= n_ctx_t, 1, 0), 0, 0)),
                  pl.BlockSpec((d, n), lambda b, i: (0, 0))],
        out_specs=pl.BlockSpec((1, ROW_TILE, n), lambda b, i: (b, i, 0)),
        out_shape=jax.ShapeDtypeStruct((bsz, length, n), F32),
        compiler_params=_cparams(("parallel", "parallel")),
    )(xs, mods, w_bf)


def _qkv_kernel(x_ref, mod_ref, w_ref, b_ref, cos_ref, sin_ref, o_ref):
    m = mod_ref[0, 0]
    h = (x_ref[0] * (1.0 + m[1:2]) + m[0:1]).astype(BF16)
    cosf = cos_ref[...]
    sinf = sin_ref[...]
    lane = lax.broadcasted_iota(jnp.int32, cosf.shape, 1)
    first_half = (lane % ATTN_HEAD_DIM) < (ATTN_HEAD_DIM // 2)
    scale = ATTN_HEAD_DIM ** -0.5
    n_rope = (Q_DIM + KV_DIM) // LANES
    for c in range(QKV_DIM // LANES):
        t = jnp.dot(h, w_ref[:, c * LANES:(c + 1) * LANES], preferred_element_type=F32)
        t = t + b_ref[:, c * LANES:(c + 1) * LANES]
        if c < n_rope:
            swapped = jnp.where(first_half,
                                pltpu.roll(t, LANES - ATTN_HEAD_DIM // 2, axis=1),
                                pltpu.roll(t, ATTN_HEAD_DIM // 2, axis=1))
            t = t * cosf + swapped * sinf
        if c < Q_DIM // LANES:
            t = t * scale
        o_ref[0, :, c * LANES:(c + 1) * LANES] = t.astype(o_ref.dtype)


def _attn_qkv(xs, mods, w_bf, b, cos_t, sin_t, n_ctx_t):
    bsz, length, d = xs.shape
    n = w_bf.shape[1]
    return pl.pallas_call(
        _qkv_kernel,
        grid=(bsz, length // ROW_TILE),
        in_specs=[pl.BlockSpec((1, ROW_TILE, d), lambda b_, i: (b_, i, 0)),
                  pl.BlockSpec((1, 1, 6, d), lambda b_, i: (b_, jnp.where(i >= n_ctx_t, 1, 0), 0, 0)),
                  pl.BlockSpec((d, n), lambda b_, i: (0, 0)),
                  pl.BlockSpec((1, n), lambda b_, i: (0, 0)),
                  pl.BlockSpec((ROW_TILE, LANES), lambda b_, i: (i, 0)),
                  pl.BlockSpec((ROW_TILE, LANES), lambda b_, i: (i, 0))],
        out_specs=pl.BlockSpec((1, ROW_TILE, n), lambda b_, i: (b_, i, 0)),
        out_shape=jax.ShapeDtypeStruct((bsz, length, n), BF16),
        compiler_params=_cparams(("parallel", "parallel")),
    )(xs, mods, w_bf, b.reshape(1, n), cos_t, sin_t)


def _conv_kernel(u_ref, prev_ref, next_ref, w_ref, b_ref, o_ref, *, n_ctx_t, n_t):
    i = pl.program_id(1)
    u = u_ref[0]
    rows = u.shape[0]
    starts_segment = jnp.logical_or(i == 0, i == n_ctx_t)
    ends_segment = jnp.logical_or(i == n_ctx_t - 1, i == n_t - 1)
    before = jnp.where(starts_segment, 0.0, prev_ref[0, SUBLANES - 1:SUBLANES, :])
    after = jnp.where(ends_segment, 0.0, next_ref[0, 0:1, :])
    row = lax.broadcasted_iota(jnp.int32, u.shape, 0)
    up = jnp.where(row == 0, before, pltpu.roll(u, 1, axis=0))
    dn = jnp.where(row == rows - 1, after, pltpu.roll(u, rows - 1, axis=0))
    w = w_ref[...]
    v = up * w[0:1] + u * w[1:2] + dn * w[2:3] + b_ref[...]
    o_ref[0] = _silu(v)


def _ssd_conv(p, conv_w, conv_b, n_ctx_t):
    bsz, length, _ = p.shape
    n_t = length // ROW_TILE
    cw = 1024
    col0 = D_INNER // cw
    r8 = ROW_TILE // SUBLANES
    last8 = length // SUBLANES - 1
    return pl.pallas_call(
        functools.partial(_conv_kernel, n_ctx_t=n_ctx_t, n_t=n_t),
        grid=(bsz, n_t, CONV_DIM // cw),
        in_specs=[pl.BlockSpec((1, ROW_TILE, cw), lambda b, i, j: (b, i, col0 + j)),
                  pl.BlockSpec((1, SUBLANES, cw), lambda b, i, j: (b, jnp.maximum(i * r8 - 1, 0), col0 + j)),
                  pl.BlockSpec((1, SUBLANES, cw), lambda b, i, j: (b, jnp.minimum((i + 1) * r8, last8), col0 + j)),
                  pl.BlockSpec((3, cw), lambda b, i, j: (0, j)),
                  pl.BlockSpec((1, cw), lambda b, i, j: (0, j))],
        out_specs=pl.BlockSpec((1, ROW_TILE, cw), lambda b, i, j: (b, i, j)),
        out_shape=jax.ShapeDtypeStruct((bsz, length, CONV_DIM), F32),
        compiler_params=_cparams(("parallel", "parallel", "parallel")),
    )(p, p, p, conv_w, conv_b.reshape(1, CONV_DIM))


def _scan_kernel(x_ref, b_ref, c_ref, dt_ref, bias_ref, alog_ref, y_ref, state_ref):
    d = pl.program_id(1)
    c = pl.program_id(2)
    fwd = d == 0

    @pl.when(c == 0)
    def _():
        state_ref[...] = jnp.zeros_like(state_ref)

    h2 = 2 * SSD_HEADS
    dt_all = _softplus(dt_ref[0][:, :h2] + bias_ref[...])
    a_all = -jnp.exp(alog_ref[...])
    dt = jnp.where(fwd, dt_all[:, :SSD_HEADS], dt_all[:, SSD_HEADS:])
    a = jnp.where(fwd, a_all[:, :SSD_HEADS], a_all[:, SSD_HEADS:])
    da = dt * a

    li = lax.broadcasted_iota(jnp.int32, (CHUNK, CHUNK), 0)
    si = lax.broadcasted_iota(jnp.int32, (CHUNK, CHUNK), 1)
    tri = jnp.where(fwd, si - li, li - si) <= 0
    tri_f = tri.astype(F32)
    cs = jnp.dot(tri_f, da, precision=HIGHEST, preferred_element_type=F32)
    cs_t = lax.dot_general(da, tri_f, (((0,), (1,)), ((), ())), precision=HIGHEST,
                           preferred_element_type=F32)
    tot = jnp.sum(da, axis=0, keepdims=True)

    fac = jnp.concatenate([dt, jnp.exp(tot - cs), jnp.exp(cs),
                           jnp.broadcast_to(jnp.exp(tot), (SUBLANES, SSD_HEADS))], axis=0)
    hi = fac.astype(BF16)
    lo = (fac - hi.astype(F32)).astype(BF16)
    hh = lax.broadcasted_iota(jnp.int32, (SSD_HEADS, D_INNER), 0)
    ch = lax.broadcasted_iota(jnp.int32, (SSD_HEADS, D_INNER), 1)
    expand = jnp.where(ch // SSD_HEAD_DIM == hh, 1.0, 0.0).astype(BF16)
    wide = (jnp.dot(hi, expand, preferred_element_type=F32)
            + jnp.dot(lo, expand, preferred_element_type=F32))
    dt_w = wide[0:CHUNK]
    to_end_w = wide[CHUNK:2 * CHUNK]
    from_start_w = wide[2 * CHUNK:3 * CHUNK]
    chunk_decay_w = wide[3 * CHUNK:3 * CHUNK + 1]

    xdt = x_ref[0] * dt_w
    xdt_b = xdt.astype(BF16)
    xend_b = (xdt * to_end_w).astype(BF16)

    for g in range(SSD_GROUPS):
        gs = slice(g * GROUP_W, (g + 1) * GROUP_W)
        bg = b_ref[0][:, g * D_STATE:(g + 1) * D_STATE].astype(BF16)
        cg = c_ref[0][:, g * D_STATE:(g + 1) * D_STATE].astype(BF16)
        cb = lax.dot_general(cg, bg, (((1,), (1,)), ((), ())), preferred_element_type=F32)
        st = state_ref[g]
        y_off = jnp.dot(cg, st.astype(BF16), preferred_element_type=F32) * from_start_w[:, gs]
        for pair in range(SSD_HPG // 2):
            parts = []
            for k in range(2):
                h = g * SSD_HPG + pair * 2 + k
                diff = cs[:, h:h + 1] - cs_t[h:h + 1, :]
                decay = jnp.exp(jnp.where(tri, diff, -jnp.inf))
                lmat = (cb * decay).astype(BF16)
                parts.append(jnp.dot(lmat, xdt_b[:, h * SSD_HEAD_DIM:(h + 1) * SSD_HEAD_DIM],
                                     preferred_element_type=F32))
            c0 = pair * 2 * SSD_HEAD_DIM
            y_ref[0, 0, :, g * GROUP_W + c0:g * GROUP_W + c0 + 2 * SSD_HEAD_DIM] = (
                jnp.concatenate(parts, axis=1) + y_off[:, c0:c0 + 2 * SSD_HEAD_DIM])
        new = lax.dot_general(bg, xend_b[:, gs], (((0,), (0,)), ((), ())), preferred_element_type=F32)
        state_ref[g] = st * chunk_decay_w[:, gs] + new


def _ssd_scan(xbc, p, dt_bias, a_log, n_ctx_c):
    bsz, length, _ = xbc.shape
    nc = length // CHUNK
    h2 = 2 * SSD_HEADS

    def chunk_of(d, c):
        back = jnp.where(c < n_ctx_c, n_ctx_c - 1 - c, nc - 1 - (c - n_ctx_c))
        return jnp.where(d == 0, c, back)

    gn = SSD_GROUPS * D_STATE
    return pl.pallas_call(
        _scan_kernel,
        grid=(bsz, 2, nc),
        in_specs=[pl.BlockSpec((1, CHUNK, D_INNER), lambda b, d, c: (b, chunk_of(d, c), 0)),
                  pl.BlockSpec((1, CHUNK, gn), lambda b, d, c: (b, chunk_of(d, c), D_INNER // gn)),
                  pl.BlockSpec((1, CHUNK, gn), lambda b, d, c: (b, chunk_of(d, c), D_INNER // gn + 1)),
                  pl.BlockSpec((1, CHUNK, LANES), lambda b, d, c: (b, chunk_of(d, c), DT_COL // LANES)),
                  pl.BlockSpec((1, h2), lambda b, d, c: (0, 0)),
                  pl.BlockSpec((1, h2), lambda b, d, c: (0, 0))],
        out_specs=pl.BlockSpec((1, 1, CHUNK, D_INNER), lambda b, d, c: (d, b, chunk_of(d, c), 0)),
        out_shape=jax.ShapeDtypeStruct((2, bsz, length, D_INNER), F32),
        scratch_shapes=[pltpu.VMEM((SSD_GROUPS, D_STATE, GROUP_W), F32)],
        compiler_params=_cparams(("parallel", "arbitrary", "arbitrary")),
    )(xbc, xbc, xbc, p, dt_bias.reshape(1, h2), a_log.reshape(1, h2))


def _finish_kernel(yf_ref, yb_ref, xs_ref, z_ref, dskip_ref, nw_ref, o_ref):
    y = yf_ref[0, 0] + yb_ref[0, 0] + xs_ref[0] * dskip_ref[...]
    g = y * _silu(z_ref[0])
    for k in range(SSD_GROUPS):
        gs = slice(k * GROUP_W, (k + 1) * GROUP_W)
        gk = g[:, gs]
        ms = jnp.mean(gk * gk, axis=-1, keepdims=True)
        o_ref[0, :, gs] = ((gk * lax.rsqrt(ms + RMS_EPS)) * nw_ref[:, gs]).astype(o_ref.dtype)


def _ssd_finish(y2, xbc, p, d_wide, norm_w):
    _, bsz, length, _ = y2.shape
    return pl.pallas_call(
        _finish_kernel,
        grid=(bsz, length // ROW_TILE),
        in_specs=[pl.BlockSpec((1, 1, ROW_TILE, D_INNER), lambda b, i: (0, b, i, 0)),
                  pl.BlockSpec((1, 1, ROW_TILE, D_INNER), lambda b, i: (1, b, i, 0)),
                  pl.BlockSpec((1, ROW_TILE, D_INNER), lambda b, i: (b, i, 0)),
                  pl.BlockSpec((1, ROW_TILE, D_INNER), lambda b, i: (b, i, 0)),
                  pl.BlockSpec((1, D_INNER), lambda b, i: (0, 0)),
                  pl.BlockSpec((1, D_INNER), lambda b, i: (0, 0))],
        out_specs=pl.BlockSpec((1, ROW_TILE, D_INNER), lambda b, i: (b, i, 0)),
        out_shape=jax.ShapeDtypeStruct((bsz, length, D_INNER), BF16),
        compiler_params=_cparams(("parallel", "parallel")),
    )(y2, y2, xbc, p, d_wide, norm_w.reshape(1, D_INNER))


def _attn_kernel(q_ref, kc_ref, vc_ref, kp_ref, ko_ref, kn_ref, vp_ref, vo_ref, vn_ref,
                 sink_ref, o_ref, *, n_blocks):
    qb = pl.program_id(1)
    n_ctx = kc_ref.shape[1]
    blk = ATTN_BLOCK
    hd = ATTN_HEAD_DIM
    rows = ATTN_GROUP * blk

    n_keys = n_ctx + 3 * blk
    qi = lax.broadcasted_iota(jnp.int32, (rows, n_keys), 0) % blk
    kj = lax.broadcasted_iota(jnp.int32, (rows, n_keys), 1) - n_ctx
    rel = kj - blk - qi
    in_window = jnp.logical_and(rel >= -blk, rel <= blk)
    in_range = jnp.logical_and(jnp.logical_or(kj >= blk, qb > 0),
                               jnp.logical_or(kj < 2 * blk, qb < n_blocks - 1))
    valid = jnp.logical_or(kj < 0, jnp.logical_and(in_window, in_range))

    q = q_ref[0]
    for k in range(ATTN_KV_HEADS):
        ks = slice(k * hd, (k + 1) * hd)
        keys = jnp.concatenate([kc_ref[0][:, ks], kp_ref[0][:, ks], ko_ref[0][:, ks], kn_ref[0][:, ks]], axis=0)
        vals = jnp.concatenate([vc_ref[0][:, ks], vp_ref[0][:, ks], vo_ref[0][:, ks], vn_ref[0][:, ks]], axis=0)
        qs = jnp.concatenate([q[:, (k * ATTN_GROUP + g) * hd:(k * ATTN_GROUP + g + 1) * hd]
                              for g in range(ATTN_GROUP)], axis=0)
        s = lax.dot_general(qs, keys, (((1,), (1,)), ((), ())), preferred_element_type=F32)
        s = jnp.where(valid, s, -jnp.inf)
        sink = sink_ref[k * rows:(k + 1) * rows, 0:1]
        m = jnp.maximum(jnp.max(s, axis=-1, keepdims=True), sink)
        e = jnp.exp(s - m)
        denom = jnp.sum(e, axis=-1, keepdims=True) + jnp.exp(sink - m)
        pr = (e / denom).astype(BF16)
        o = jnp.dot(pr, vals, preferred_element_type=F32)
        for pair in range(ATTN_GROUP // 2):
            g0 = 2 * pair
            both = jnp.concatenate([o[g0 * blk:(g0 + 1) * blk], o[(g0 + 1) * blk:(g0 + 2) * blk]], axis=1)
            c0 = (k * ATTN_GROUP + g0) * hd
            o_ref[0, :, c0:c0 + 2 * hd] = both.astype(o_ref.dtype)


def _attention(qkv, sink_rows, n_ctx):
    bsz, length, _ = qkv.shape
    n_lat = length - n_ctx
    nb = n_lat // ATTN_BLOCK
    cb = n_ctx // ATTN_BLOCK
    kcol = Q_DIM // KV_DIM
    vcol = kcol + 1

    def band(col, off):
        return pl.BlockSpec(
            (1, ATTN_BLOCK, KV_DIM),
            lambda b, i: (b, cb + jnp.clip(i + off, 0, nb - 1), col))

    return pl.pallas_call(
        functools.partial(_attn_kernel, n_blocks=nb),
        grid=(bsz, nb),
        in_specs=[pl.BlockSpec((1, ATTN_BLOCK, Q_DIM), lambda b, i: (b, cb + i, 0)),
                  pl.BlockSpec((1, n_ctx, KV_DIM), lambda b, i: (b, 0, kcol)),
                  pl.BlockSpec((1, n_ctx, KV_DIM), lambda b, i: (b, 0, vcol)),
                  band(kcol, -1), band(kcol, 0), band(kcol, 1),
                  band(vcol, -1), band(vcol, 0), band(vcol, 1),
                  pl.BlockSpec(sink_rows.shape, lambda b, i: (0, 0))],
        out_specs=pl.BlockSpec((1, ATTN_BLOCK, Q_DIM), lambda b, i: (b, i, 0)),
        out_shape=jax.ShapeDtypeStruct((bsz, n_lat, Q_DIM), BF16),
        compiler_params=_cparams(("parallel", "parallel")),
    )(qkv, qkv, qkv, qkv, qkv, qkv, qkv, qkv, qkv, sink_rows)


def _layer_norm(v, g, b):
    mu = jnp.mean(v, axis=-1, keepdims=True)
    cen = v - mu
    var = jnp.mean(cen * cen, axis=-1, keepdims=True)
    return cen * lax.rsqrt(var + LN_EPS) * g + b


def _proj_ln_kernel(a_ref, w_ref, bo_ref, x_ref, mod_ref, lng_ref, lnb_ref, wr_ref, br_ref,
                    x1_ref, tok_ref, ti_ref, tw_ref, rank_ref, cnt_ref, carry_ref):
    first = jnp.logical_and(pl.program_id(0) == 0, pl.program_id(1) == 0)

    @pl.when(first)
    def _():
        carry_ref[...] = jnp.zeros_like(carry_ref)

    m = mod_ref[0, 0]
    y = jnp.dot(a_ref[0], w_ref[...], preferred_element_type=F32) + bo_ref[...]
    x1 = _layer_norm(ALPHA * x_ref[0] + m[2:3] * y, lng_ref[...], lnb_ref[...])
    x1_ref[0] = x1
    tok = x1 * (1.0 + m[4:5]) + m[3:4]
    tok_ref[0] = tok
    logits = jnp.dot(tok, wr_ref[...], precision=HIGHEST, preferred_element_type=F32) + br_ref[...]
    lane = lax.broadcasted_iota(jnp.int32, logits.shape, 1)
    vals = logits
    top_v, top_i = [], []
    for _ in range(TOP_K):
        mx = jnp.max(vals, axis=-1, keepdims=True)
        idx = jnp.min(jnp.where(vals == mx, lane, N_EXPERTS), axis=-1, keepdims=True)
        vals = jnp.where(lane == idx, -jnp.inf, vals)
        top_v.append(mx)
        top_i.append(idx)
    ex = [jnp.exp(v - top_v[0]) for v in top_v]
    den = ex[0] + ex[1] + ex[2] + ex[3]
    ti_ref[0] = jnp.concatenate(top_i, axis=1)
    tw_ref[0] = jnp.concatenate([e / den for e in ex], axis=1)

    rows = logits.shape[0]
    picked = [lane == idx for idx in top_i]
    onehot = jnp.where(picked[0] | picked[1] | picked[2] | picked[3], 1.0, 0.0)
    ri = lax.broadcasted_iota(jnp.int32, (rows, rows), 0)
    ci = lax.broadcasted_iota(jnp.int32, (rows, rows), 1)
    earlier = jnp.where(ci < ri, 1.0, 0.0).astype(BF16)
    before = jnp.dot(earlier, onehot.astype(BF16), preferred_element_type=F32) + carry_ref[0:1, :]
    ranks = [jnp.sum(jnp.where(p, before, 0.0), axis=-1, keepdims=True) for p in picked]
    rank_ref[0] = jnp.concatenate(ranks, axis=1).astype(jnp.int32)
    total = carry_ref[0:1, :] + jnp.sum(onehot, axis=0, keepdims=True)
    carry_ref[...] = jnp.broadcast_to(total, carry_ref.shape)
    cnt_ref[...] = jnp.broadcast_to(total, cnt_ref.shape).astype(jnp.int32)


def _proj_ln(a, w_bf, b_o, xs, mods, ln_g, ln_b, w_r, b_r, *, row_off_t, n_ctx_t):
    bsz, rows, kdim = a.shape
    d = xs.shape[2]

    def mod_idx(b, i):
        return (b, jnp.where(i + row_off_t >= n_ctx_t, 1, 0), 0, 0)

    row_spec = lambda width: pl.BlockSpec((1, ROW_TILE, width), lambda b, i: (b, i, 0))
    vec_spec = pl.BlockSpec((1, d), lambda b, i: (0, 0))
    return pl.pallas_call(
        _proj_ln_kernel,
        grid=(bsz, rows // ROW_TILE),
        in_specs=[row_spec(kdim),
                  pl.BlockSpec((kdim, d), lambda b, i: (0, 0)),
                  vec_spec,
                  pl.BlockSpec((1, ROW_TILE, d), lambda b, i: (b, i + row_off_t, 0)),
                  pl.BlockSpec((1, 1, 6, d), mod_idx),
                  vec_spec, vec_spec,
                  pl.BlockSpec((d, N_EXPERTS), lambda b, i: (0, 0)),
                  pl.BlockSpec((1, N_EXPERTS), lambda b, i: (0, 0))],
        out_specs=[row_spec(d), row_spec(d), row_spec(TOP_K), row_spec(TOP_K), row_spec(TOP_K),
                   pl.BlockSpec((SUBLANES, N_EXPERTS), lambda b, i: (0, 0))],
        out_shape=[jax.ShapeDtypeStruct((bsz, rows, d), F32),
                   jax.ShapeDtypeStruct((bsz, rows, d), F32),
                   jax.ShapeDtypeStruct((bsz, rows, TOP_K), jnp.int32),
                   jax.ShapeDtypeStruct((bsz, rows, TOP_K), F32),
                   jax.ShapeDtypeStruct((bsz, rows, TOP_K), jnp.int32),
                   jax.ShapeDtypeStruct((SUBLANES, N_EXPERTS), jnp.int32)],
        scratch_shapes=[pltpu.VMEM((SUBLANES, N_EXPERTS), F32)],
        compiler_params=_cparams(("arbitrary", "arbitrary")),
    )(a, w_bf, b_o.reshape(1, d), xs, mods, ln_g.reshape(1, d), ln_b.reshape(1, d),
      w_r, b_r.reshape(1, N_EXPERTS))


def _dispatch_kernel(slot_ref, tok_ref, init_ref, xg_ref, sem):
    del init_ref
    rows = tok_ref.shape[0]

    def row_copy(r, k):
        return pltpu.make_async_copy(tok_ref.at[pl.ds(r, 1)], xg_ref.at[pl.ds(slot_ref[0, 0, r * TOP_K + k], 1)], sem)

    def issue(r, carry):
        for k in range(TOP_K):
            row_copy(r, k).start()
        return carry

    lax.fori_loop(0, rows, issue, 0)

    def drain(r, carry):
        for k in range(TOP_K):
            row_copy(r, k).wait()
        return carry

    lax.fori_loop(0, rows, drain, 0)


def _dispatch(slots, tok, n_slots):
    t, d = tok.shape
    nt = t // ROW_TILE
    return pl.pallas_call(
        _dispatch_kernel,
        grid=(nt,),
        in_specs=[pl.BlockSpec((1, 1, ROW_TILE * TOP_K), lambda i: (i, 0, 0), memory_space=pltpu.SMEM),
                  pl.BlockSpec((ROW_TILE, d), lambda i: (i, 0)),
                  pl.BlockSpec(memory_space=pl.ANY)],
        out_specs=pl.BlockSpec(memory_space=pl.ANY),
        out_shape=jax.ShapeDtypeStruct((n_slots, d), F32),
        scratch_shapes=[pltpu.SemaphoreType.DMA(())],
        input_output_aliases={2: 0},
        compiler_params=pltpu.CompilerParams(dimension_semantics=("arbitrary",), vmem_limit_bytes=VMEM_LIMIT,
                                             disable_bounds_checks=True),
    )(slots, tok, jnp.zeros((n_slots, d), F32))


def _expert_kernel(te_ref, tv_ref, tf_ref, x_ref, w1_ref, b1_ref, w2_ref, b2_ref, y_ref,
                   w1b_ref, w2f_ref, w2b_ref):
    i = pl.program_id(0)
    f2 = w1_ref.shape[2]
    prep_rows = 128

    @pl.when(i == 0)
    def _():
        w2f_ref[...] = jnp.zeros_like(w2f_ref)

    @pl.when(tf_ref[i] > 0)
    def _():
        def cast_w1(j, carry):
            r = pl.multiple_of(j * prep_rows, prep_rows)
            w1b_ref[pl.ds(r, prep_rows), :] = w1_ref[0, pl.ds(r, prep_rows), :].astype(BF16)
            return carry

        lax.fori_loop(0, w1_ref.shape[1] // prep_rows, cast_w1, 0)
        for c in range(w2_ref.shape[2] // LANES):
            cs = slice(c * LANES, (c + 1) * LANES)
            w2f_ref[c, pl.ds(0, f2 // 2, stride=2), :] = w2_ref[0, :, cs]
            w2b_ref[:, cs] = w2f_ref[c].astype(BF16)

    @pl.when(tv_ref[i] > 0)
    def _():
        x = x_ref[...].astype(BF16)
        even = lax.broadcasted_iota(jnp.int32, (x.shape[0], LANES), 1) % 2 == 0
        acts = []
        for c in range(f2 // LANES):
            cs = slice(c * LANES, (c + 1) * LANES)
            h = jnp.dot(x, w1b_ref[:, cs], preferred_element_type=F32) + b1_ref[0][:, cs]
            glu = jnp.minimum(h, SWIGLU_LIMIT)
            lin = jnp.clip(h, -SWIGLU_LIMIT, SWIGLU_LIMIT)
            gate = glu * (1.0 / (1.0 + jnp.exp(-SWIGLU_ALPHA * glu)))
            nxt = pltpu.roll(lin, LANES - 1, axis=1)
            acts.append(jnp.where(even, gate * (nxt + 1.0), 0.0).astype(BF16))
        act = jnp.concatenate(acts, axis=1)
        y_ref[...] = jnp.dot(act, w2b_ref[...], preferred_element_type=F32) + b2_ref[0]

    @pl.when(tv_ref[i] == 0)
    def _():
        y_ref[...] = jnp.zeros_like(y_ref)


def _experts(tile_e, tile_v, tile_f, xg, w1, b1, w2, b2):
    ns, d = xg.shape
    f2 = w1.shape[2]
    nt = ns // MOE_TILE
    grid_spec = pltpu.PrefetchScalarGridSpec(
        num_scalar_prefetch=3,
        grid=(nt,),
        in_specs=[pl.BlockSpec((MOE_TILE, d), lambda i, te, tv, tf: (i, 0)),
                  pl.BlockSpec((1, d, f2), lambda i, te, tv, tf: (te[i], 0, 0)),
                  pl.BlockSpec((1, 1, f2), lambda i, te, tv, tf: (te[i], 0, 0)),
                  pl.BlockSpec((1, f2 // 2, d), lambda i, te, tv, tf: (te[i], 0, 0)),
                  pl.BlockSpec((1, 1, d), lambda i, te, tv, tf: (te[i], 0, 0))],
        out_specs=pl.BlockSpec((MOE_TILE, d), lambda i, te, tv, tf: (i, 0)),
        scratch_shapes=[pltpu.VMEM((d, f2), BF16), pltpu.VMEM((d // LANES, f2, LANES), F32),
                        pltpu.VMEM((f2, d), BF16)],
    )
    return pl.pallas_call(
        _expert_kernel,
        grid_spec=grid_spec,
        out_shape=jax.ShapeDtypeStruct((ns, d), F32),
        compiler_params=_cparams(("arbitrary",)),
    )(tile_e, tile_v, tile_f, xg, w1, b1.reshape(N_EXPERTS, 1, f2), w2, b2.reshape(N_EXPERTS, 1, d))


def _combine_kernel(slot_ref, tw_ref, x_ref, mod_ref, lng_ref, lnb_ref, y_ref, o_ref, buf_ref, sem):
    rows = x_ref.shape[1]

    def row_copy(r, k):
        return pltpu.make_async_copy(y_ref.at[pl.ds(slot_ref[0, 0, r * TOP_K + k], 1)],
                                     buf_ref.at[k, pl.ds(r, 1)], sem)

    def issue(r, carry):
        for k in range(TOP_K):
            row_copy(r, k).start()
        return carry

    lax.fori_loop(0, rows, issue, 0)

    def drain(r, carry):
        for k in range(TOP_K):
            row_copy(r, k).wait()
        return carry

    lax.fori_loop(0, rows, drain, 0)

    m = mod_ref[0, 0]
    tw = tw_ref[0]
    f = tw[:, 0:1] * buf_ref[0]
    for k in range(1, TOP_K):
        f = f + tw[:, k:k + 1] * buf_ref[k]
    o_ref[0] = _layer_norm(ALPHA * x_ref[0] + m[5:6] * f, lng_ref[...], lnb_ref[...])


def _combine(slots, y, tw, x1, mods, ln_g, ln_b, *, row_off_t, n_ctx_t):
    bsz, rows, d = x1.shape
    n_t = rows // ROW_TILE

    def mod_idx(b, i):
        return (b, jnp.where(i + row_off_t >= n_ctx_t, 1, 0), 0, 0)

    vec_spec = pl.BlockSpec((1, d), lambda b, i: (0, 0))
    return pl.pallas_call(
        _combine_kernel,
        grid=(bsz, n_t),
        in_specs=[pl.BlockSpec((1, 1, ROW_TILE * TOP_K), lambda b, i: (b * n_t + i, 0, 0), memory_space=pltpu.SMEM),
                  pl.BlockSpec((1, ROW_TILE, TOP_K), lambda b, i: (b, i, 0)),
                  pl.BlockSpec((1, ROW_TILE, d), lambda b, i: (b, i, 0)),
                  pl.BlockSpec((1, 1, 6, d), mod_idx),
                  vec_spec, vec_spec,
                  pl.BlockSpec(memory_space=pl.ANY)],
        out_specs=pl.BlockSpec((1, ROW_TILE, d), lambda b, i: (b, i, 0)),
        out_shape=jax.ShapeDtypeStruct((bsz, rows, d), F32),
        scratch_shapes=[pltpu.VMEM((TOP_K, ROW_TILE, d), F32), pltpu.SemaphoreType.DMA(())],
        compiler_params=pltpu.CompilerParams(dimension_semantics=("arbitrary", "arbitrary"),
                                             vmem_limit_bytes=VMEM_LIMIT, disable_bounds_checks=True),
    )(slots, tw, x1, mods, ln_g.reshape(1, d), ln_b.reshape(1, d), y)


def _moe_layout(top_i, rank, counts):
    n_assign = top_i.size
    padded = ((counts + MOE_TILE - 1) // MOE_TILE) * MOE_TILE
    pad_end = jnp.cumsum(padded)
    pad_off = pad_end - padded
    experts = jnp.arange(N_EXPERTS, dtype=jnp.int32)
    base = jnp.sum(jnp.where(top_i[..., None] == experts, pad_off, 0), axis=-1)
    slots = (base + rank).astype(jnp.int32).reshape(n_assign // (ROW_TILE * TOP_K), 1, ROW_TILE * TOP_K)
    n_tiles = n_assign // MOE_TILE + N_EXPERTS
    tile_start = jnp.arange(n_tiles, dtype=jnp.int32) * MOE_TILE
    owner = jnp.sum((tile_start[:, None] >= pad_end[None, :]).astype(jnp.int32), axis=1)
    tile_v = (owner < N_EXPERTS).astype(jnp.int32)
    tile_e = jnp.minimum(owner, N_EXPERTS - 1)
    tile_f = tile_v * jnp.sum(((tile_start[:, None] == pad_off[None, :]) & (padded[None, :] > 0)).astype(jnp.int32),
                              axis=1)
    return slots, tile_e, tile_v, tile_f, n_tiles * MOE_TILE


def _rope_tables(n_lat, n_ctx):
    rows = n_lat // GRID_W
    row = jnp.repeat(jnp.arange(rows), GRID_W).astype(F32)
    col = jnp.broadcast_to(jnp.arange(GRID_W)[None, :], (rows, GRID_W)).reshape(-1).astype(F32)
    n_freq = ATTN_HEAD_DIM // 4
    inv = jnp.power(ROPE_BASE, -jnp.arange(n_freq, dtype=F32) / n_freq)
    ang = jnp.concatenate([row[:, None] * inv, col[:, None] * inv], axis=-1)
    cos, sin = jnp.cos(ang), jnp.sin(ang)
    cos_h = jnp.concatenate([cos, cos], axis=-1)
    sin_h = jnp.concatenate([-sin, sin], axis=-1)
    reps = LANES // ATTN_HEAD_DIM
    cos_t = jnp.concatenate([jnp.ones((n_ctx, LANES), F32), jnp.tile(cos_h, (1, reps))], axis=0)
    sin_t = jnp.concatenate([jnp.zeros((n_ctx, LANES), F32), jnp.tile(sin_h, (1, reps))], axis=0)
    return cos_t, sin_t


def kernel(x, c, ctx, c_ctx, ada_w, ada_b, ln_g, ln_b, ssd_w_in, ssd_conv_w, ssd_conv_b, ssd_dt_bias,
           ssd_a_log, ssd_d, ssd_norm_w, ssd_w_out, attn_w_qkv, attn_b_qkv, attn_sinks, attn_w_o, attn_b_o,
           router_w, router_b, moe_w1, moe_b1, moe_w2, moe_b2):
    bsz, n_lat, d = x.shape
    n_ctx = ctx.shape[1]
    assert d == D_MODEL and bsz + 1 <= SUBLANES
    assert n_ctx % ROW_TILE == 0 and n_lat % ROW_TILE == 0 and n_lat % GRID_W == 0
    n_ctx_t = n_ctx // ROW_TILE

    cond = jnp.concatenate([c, c_ctx[None], jnp.zeros((SUBLANES - bsz - 1, d), F32)], axis=0)
    ada = _ada(cond, ada_w, ada_b)
    xs = jnp.concatenate([ctx, x], axis=1)

    for i in range(DEPTH):
        last = i == DEPTH - 1
        m_lat = ada[i, :bsz].reshape(bsz, 1, 6, d)
        m_ctx = jnp.broadcast_to(ada[i, bsz].reshape(1, 1, 6, d), (bsz, 1, 6, d))
        mods = jnp.concatenate([m_ctx, m_lat], axis=1)
        j = i // 2
        if i % 2 == 0:
            p = _ssd_inproj(xs, mods, ssd_w_in[j].astype(BF16), n_ctx_t)
            xbc = _ssd_conv(p, ssd_conv_w[j], ssd_conv_b[j], n_ctx_t)
            y2 = _ssd_scan(xbc, p, ssd_dt_bias[j], ssd_a_log[j], n_ctx // CHUNK)
            d_wide = jnp.repeat(ssd_d[j], SSD_HEAD_DIM).reshape(1, D_INNER)
            mix = _ssd_finish(y2, xbc, p, d_wide, ssd_norm_w[j])
            w_o, b_o = ssd_w_out[j].astype(BF16), jnp.zeros((d,), F32)
            if last:
                mix = mix[:, n_ctx:]
        else:
            cos_t, sin_t = _rope_tables(n_lat, n_ctx)
            qkv = _attn_qkv(xs, mods, attn_w_qkv[j].astype(BF16), attn_b_qkv[j], cos_t, sin_t, n_ctx_t)
            sink_rows = jnp.broadcast_to(
                jnp.repeat(attn_sinks[j].astype(F32), ATTN_BLOCK)[:, None], (ATTN_HEADS * ATTN_BLOCK, LANES))
            mix = _attention(qkv, sink_rows, n_ctx)
            w_o, b_o = attn_w_o[j].astype(BF16), attn_b_o[j]
            if not last:
                raise NotImplementedError("context queries are only needed when an attention layer is not last")
        row_off_t = n_ctx_t if last else 0
        x1, tok, top_i, top_w, rank, counts = _proj_ln(mix, w_o, b_o, xs, mods, ln_g[i, 0], ln_b[i, 0],
                                                       router_w[i], router_b[i], row_off_t=row_off_t, n_ctx_t=n_ctx_t)
        slots, tile_e, tile_v, tile_f, n_slots = _moe_layout(top_i, rank, counts[0])
        xg = _dispatch(slots, tok.reshape(-1, d), n_slots)
        y = _experts(tile_e, tile_v, tile_f, xg, moe_w1[i], moe_b1[i], moe_w2[i], moe_b2[i])
        xs = _combine(slots, y, top_w, x1, mods, ln_g[i, 1], ln_b[i, 1], row_off_t=row_off_t, n_ctx_t=n_ctx_t)
    return xs
```

```python
import functools

import jax
import jax.numpy as jnp
from jax import lax
from jax.experimental import pallas as pl
from jax.experimental.pallas import tpu as pltpu

F32 = jnp.float32
BF16 = jnp.bfloat16
HIGHEST = lax.Precision.HIGHEST

D_MODEL = 1024
DEPTH = 2
GRID_W = 64
ALPHA = (2.0 * DEPTH) ** 0.25
LN_EPS = 1e-5
RMS_EPS = 1e-5

D_INNER = 2 * D_MODEL
SSD_HEAD_DIM = 64
SSD_HEADS = D_INNER // SSD_HEAD_DIM
SSD_GROUPS = 4
SSD_HPG = SSD_HEADS // SSD_GROUPS
D_STATE = 128
GROUP_W = SSD_HPG * SSD_HEAD_DIM
CONV_DIM = D_INNER + 2 * SSD_GROUPS * D_STATE
SSD_IN_DIM = D_INNER + CONV_DIM + 2 * SSD_HEADS
DT_COL = D_INNER + CONV_DIM
CHUNK = 128

ATTN_HEADS = 16
ATTN_KV_HEADS = 4
ATTN_GROUP = ATTN_HEADS // ATTN_KV_HEADS
ATTN_HEAD_DIM = 64
Q_DIM = ATTN_HEADS * ATTN_HEAD_DIM
KV_DIM = ATTN_KV_HEADS * ATTN_HEAD_DIM
QKV_DIM = Q_DIM + 2 * KV_DIM
ATTN_BLOCK = 128
ROPE_BASE = 10000.0

N_EXPERTS = 32
TOP_K = 4
D_FF = D_MODEL
SWIGLU_ALPHA = 1.702
SWIGLU_LIMIT = 7.0

LANES = 128
SUBLANES = 8
MXU_COLS = 256
ROW_TILE = 256
MOE_TILE = 256
VMEM_LIMIT = 56 * 1024 * 1024


def _cparams(sem):
    return pltpu.CompilerParams(dimension_semantics=sem, vmem_limit_bytes=VMEM_LIMIT)


def _silu(v):
    return v * (1.0 / (1.0 + jnp.exp(-v)))


def _softplus(v):
    return jnp.maximum(v, 0.0) + jnp.log1p(jnp.exp(-jnp.abs(v)))


def _ada_kernel(c_ref, w_ref, b_ref, o_ref):
    a = _silu(c_ref[...])
    o_ref[0] = jnp.dot(a, w_ref[0], precision=HIGHEST, preferred_element_type=F32) + b_ref[0]


def _ada(cond, ada_w, ada_b):
    depth, d, n = ada_w.shape
    tn = 1536
    return pl.pallas_call(
        _ada_kernel,
        grid=(depth, n // tn),
        in_specs=[pl.BlockSpec((SUBLANES, d), lambda l, j: (0, 0)),
                  pl.BlockSpec((1, d, tn), lambda l, j: (l, 0, j)),
                  pl.BlockSpec((1, 1, tn), lambda l, j: (l, 0, j))],
        out_specs=pl.BlockSpec((1, SUBLANES, tn), lambda l, j: (l, 0, j)),
        out_shape=jax.ShapeDtypeStruct((depth, SUBLANES, n), F32),
        compiler_params=_cparams(("parallel", "parallel")),
    )(cond, ada_w, ada_b.reshape(depth, 1, n))


def _inproj_kernel(x_ref, mod_ref, w_ref, o_ref, *, col_chunk):
    m = mod_ref[0, 0]
    h = (x_ref[0] * (1.0 + m[1:2]) + m[0:1]).astype(BF16)
    n = w_ref.shape[1]
    for c0 in range(0, n, col_chunk):
        c1 = min(c0 + col_chunk, n)
        o_ref[0, :, c0:c1] = jnp.dot(h, w_ref[:, c0:c1], preferred_element_type=F32)


def _ssd_inproj(xs, mods, w_bf, n_ctx_t):
    bsz, length, d = xs.shape
    n = w_bf.shape[1]
    return pl.pallas_call(
        functools.partial(_inproj_kernel, col_chunk=1024),
        grid=(bsz, length // ROW_TILE),
        in_specs=[pl.BlockSpec((1, ROW_TILE, d), lambda b, i: (b, i, 0)),
                  pl.BlockSpec((1, 1, 6, d), lambda b, i: (b, jnp.where(i >= n_ctx_t, 1, 0), 0, 0)),
                  pl.BlockSpec((d, n), lambda b, i: (0, 0))],
        out_specs=pl.BlockSpec((1, ROW_TILE, n), lambda b, i: (b, i, 0)),
        out_shape=jax.ShapeDtypeStruct((bsz, length, n), F32),
        compiler_params=_cparams(("parallel", "parallel")),
    )(xs, mods, w_bf)


def _qkv_kernel(x_ref, mod_ref, w_ref, b_ref, cos_ref, sin_ref, o_ref):
    m = mod_ref[0, 0]
    h = (x_ref[0] * (1.0 + m[1:2]) + m[0:1]).astype(BF16)
    cosf = cos_ref[...]
    sinf = sin_ref[...]
    lane = lax.broadcasted_iota(jnp.int32, cosf.shape, 1)
    first_half = (lane % ATTN_HEAD_DIM) < (ATTN_HEAD_DIM // 2)
    scale = ATTN_HEAD_DIM ** -0.5
    n_rope = (Q_DIM + KV_DIM) // LANES
    for c0 in range(0, QKV_DIM, MXU_COLS):
        wide = jnp.dot(h, w_ref[:, c0:c0 + MXU_COLS], preferred_element_type=F32) + b_ref[:, c0:c0 + MXU_COLS]
        for c in range(c0 // LANES, (c0 + MXU_COLS) // LANES):
            t = wide[:, c * LANES - c0:(c + 1) * LANES - c0]
            if c < n_rope:
                swapped = jnp.where(first_half,
                                    pltpu.roll(t, LANES - ATTN_HEAD_DIM // 2, axis=1),
                                    pltpu.roll(t, ATTN_HEAD_DIM // 2, axis=1))
                t = t * cosf + swapped * sinf
            if c < Q_DIM // LANES:
                t = t * scale
            o_ref[0, :, c * LANES:(c + 1) * LANES] = t.astype(o_ref.dtype)


def _attn_qkv(xs, mods, w_bf, b, cos_t, sin_t, n_ctx_t):
    bsz, length, d = xs.shape
    n = w_bf.shape[1]
    return pl.pallas_call(
        _qkv_kernel,
        grid=(bsz, length // ROW_TILE),
        in_specs=[pl.BlockSpec((1, ROW_TILE, d), lambda b_, i: (b_, i, 0)),
                  pl.BlockSpec((1, 1, 6, d), lambda b_, i: (b_, jnp.where(i >= n_ctx_t, 1, 0), 0, 0)),
                  pl.BlockSpec((d, n), lambda b_, i: (0, 0)),
                  pl.BlockSpec((1, n), lambda b_, i: (0, 0)),
                  pl.BlockSpec((ROW_TILE, LANES), lambda b_, i: (i, 0)),
                  pl.BlockSpec((ROW_TILE, LANES), lambda b_, i: (i, 0))],
        out_specs=pl.BlockSpec((1, ROW_TILE, n), lambda b_, i: (b_, i, 0)),
        out_shape=jax.ShapeDtypeStruct((bsz, length, n), BF16),
        compiler_params=_cparams(("parallel", "parallel")),
    )(xs, mods, w_bf, b.reshape(1, n), cos_t, sin_t)


def _conv_kernel(u_ref, prev_ref, next_ref, w_ref, b_ref, o_ref, *, n_ctx_t, n_t):
    i = pl.program_id(1)
    u = u_ref[0]
    rows = u.shape[0]
    starts_segment = jnp.logical_or(i == 0, i == n_ctx_t)
    ends_segment = jnp.logical_or(i == n_ctx_t - 1, i == n_t - 1)
    before = jnp.where(starts_segment, 0.0, prev_ref[0, SUBLANES - 1:SUBLANES, :])
    after = jnp.where(ends_segment, 0.0, next_ref[0, 0:1, :])
    row = lax.broadcasted_iota(jnp.int32, u.shape, 0)
    up = jnp.where(row == 0, before, pltpu.roll(u, 1, axis=0))
    dn = jnp.where(row == rows - 1, after, pltpu.roll(u, rows - 1, axis=0))
    w = w_ref[...]
    v = up * w[0:1] + u * w[1:2] + dn * w[2:3] + b_ref[...]
    o_ref[0] = _silu(v)


def _ssd_conv(p, conv_w, conv_b, n_ctx_t):
    bsz, length, _ = p.shape
    n_t = length // ROW_TILE
    cw = 1024
    col0 = D_INNER // cw
    r8 = ROW_TILE // SUBLANES
    last8 = length // SUBLANES - 1
    return pl.pallas_call(
        functools.partial(_conv_kernel, n_ctx_t=n_ctx_t, n_t=n_t),
        grid=(bsz, n_t, CONV_DIM // cw),
        in_specs=[pl.BlockSpec((1, ROW_TILE, cw), lambda b, i, j: (b, i, col0 + j)),
                  pl.BlockSpec((1, SUBLANES, cw), lambda b, i, j: (b, jnp.maximum(i * r8 - 1, 0), col0 + j)),
                  pl.BlockSpec((1, SUBLANES, cw), lambda b, i, j: (b, jnp.minimum((i + 1) * r8, last8), col0 + j)),
                  pl.BlockSpec((3, cw), lambda b, i, j: (0, j)),
                  pl.BlockSpec((1, cw), lambda b, i, j: (0, j))],
        out_specs=pl.BlockSpec((1, ROW_TILE, cw), lambda b, i, j: (b, i, j)),
        out_shape=jax.ShapeDtypeStruct((bsz, length, CONV_DIM), F32),
        compiler_params=_cparams(("parallel", "parallel", "parallel")),
    )(p, p, p, conv_w, conv_b.reshape(1, CONV_DIM))


def _scan_kernel(x_ref, b_ref, c_ref, dt_ref, bias_ref, alog_ref, y_ref, state_ref):
    d = pl.program_id(1)
    c = pl.program_id(2)
    fwd = d == 0

    @pl.when(c == 0)
    def _():
        state_ref[...] = jnp.zeros_like(state_ref)

    h2 = 2 * SSD_HEADS
    dt_all = _softplus(dt_ref[0][:, :h2] + bias_ref[...])
    a_all = -jnp.exp(alog_ref[...])
    dt = jnp.where(fwd, dt_all[:, :SSD_HEADS], dt_all[:, SSD_HEADS:])
    a = jnp.where(fwd, a_all[:, :SSD_HEADS], a_all[:, SSD_HEADS:])
    da = dt * a

    li = lax.broadcasted_iota(jnp.int32, (CHUNK, CHUNK), 0)
    si = lax.broadcasted_iota(jnp.int32, (CHUNK, CHUNK), 1)
    tri = jnp.where(fwd, si - li, li - si) <= 0
    tri_f = tri.astype(F32)
    cs = jnp.dot(tri_f, da, precision=HIGHEST, preferred_element_type=F32)
    cs_t = lax.dot_general(da, tri_f, (((0,), (1,)), ((), ())), precision=HIGHEST,
                           preferred_element_type=F32)
    tot = jnp.sum(da, axis=0, keepdims=True)

    fac = jnp.concatenate([dt, jnp.exp(tot - cs), jnp.exp(cs),
                           jnp.broadcast_to(jnp.exp(tot), (SUBLANES, SSD_HEADS))], axis=0)
    hi = fac.astype(BF16)
    lo = (fac - hi.astype(F32)).astype(BF16)
    hh = lax.broadcasted_iota(jnp.int32, (SSD_HEADS, D_INNER), 0)
    ch = lax.broadcasted_iota(jnp.int32, (SSD_HEADS, D_INNER), 1)
    expand = jnp.where(ch // SSD_HEAD_DIM == hh, 1.0, 0.0).astype(BF16)
    wide = (jnp.dot(hi, expand, preferred_element_type=F32)
            + jnp.dot(lo, expand, preferred_element_type=F32))
    dt_w = wide[0:CHUNK]
    to_end_w = wide[CHUNK:2 * CHUNK]
    from_start_w = wide[2 * CHUNK:3 * CHUNK]
    chunk_decay_w = wide[3 * CHUNK:3 * CHUNK + 1]

    xdt = x_ref[0] * dt_w
    xdt_b = xdt.astype(BF16)
    xend_b = (xdt * to_end_w).astype(BF16)

    for g in range(SSD_GROUPS):
        gs = slice(g * GROUP_W, (g + 1) * GROUP_W)
        bg = b_ref[0][:, g * D_STATE:(g + 1) * D_STATE].astype(BF16)
        cg = c_ref[0][:, g * D_STATE:(g + 1) * D_STATE].astype(BF16)
        cb = lax.dot_general(cg, bg, (((1,), (1,)), ((), ())), preferred_element_type=F32)
        st = state_ref[g]
        y_off = jnp.dot(cg, st.astype(BF16), preferred_element_type=F32) * from_start_w[:, gs]
        for pair in range(SSD_HPG // 2):
            parts = []
            for k in range(2):
                h = g * SSD_HPG + pair * 2 + k
                diff = cs[:, h:h + 1] - cs_t[h:h + 1, :]
                decay = jnp.exp(jnp.where(tri, diff, -jnp.inf))
                lmat = (cb * decay).astype(BF16)
                parts.append(jnp.dot(lmat, xdt_b[:, h * SSD_HEAD_DIM:(h + 1) * SSD_HEAD_DIM],
                                     preferred_element_type=F32))
            c0 = pair * 2 * SSD_HEAD_DIM
            y_ref[0, 0, :, g * GROUP_W + c0:g * GROUP_W + c0 + 2 * SSD_HEAD_DIM] = (
                jnp.concatenate(parts, axis=1) + y_off[:, c0:c0 + 2 * SSD_HEAD_DIM])
        new = lax.dot_general(bg, xend_b[:, gs], (((0,), (0,)), ((), ())), preferred_element_type=F32)
        state_ref[g] = st * chunk_decay_w[:, gs] + new


def _ssd_scan(xbc, p, dt_bias, a_log, n_ctx_c):
    bsz, length, _ = xbc.shape
    nc = length // CHUNK
    h2 = 2 * SSD_HEADS

    def chunk_of(d, c):
        back = jnp.where(c < n_ctx_c, n_ctx_c - 1 - c, nc - 1 - (c - n_ctx_c))
        return jnp.where(d == 0, c, back)

    gn = SSD_GROUPS * D_STATE
    return pl.pallas_call(
        _scan_kernel,
        grid=(bsz, 2, nc),
        in_specs=[pl.BlockSpec((1, CHUNK, D_INNER), lambda b, d, c: (b, chunk_of(d, c), 0)),
                  pl.BlockSpec((1, CHUNK, gn), lambda b, d, c: (b, chunk_of(d, c), D_INNER // gn)),
                  pl.BlockSpec((1, CHUNK, gn), lambda b, d, c: (b, chunk_of(d, c), D_INNER // gn + 1)),
                  pl.BlockSpec((1, CHUNK, LANES), lambda b, d, c: (b, chunk_of(d, c), DT_COL // LANES)),
                  pl.BlockSpec((1, h2), lambda b, d, c: (0, 0)),
                  pl.BlockSpec((1, h2), lambda b, d, c: (0, 0))],
        out_specs=pl.BlockSpec((1, 1, CHUNK, D_INNER), lambda b, d, c: (d, b, chunk_of(d, c), 0)),
        out_shape=jax.ShapeDtypeStruct((2, bsz, length, D_INNER), F32),
        scratch_shapes=[pltpu.VMEM((SSD_GROUPS, D_STATE, GROUP_W), F32)],
        compiler_params=_cparams(("parallel", "arbitrary", "arbitrary")),
    )(xbc, xbc, xbc, p, dt_bias.reshape(1, h2), a_log.reshape(1, h2))


def _finish_kernel(yf_ref, yb_ref, xs_ref, z_ref, dskip_ref, nw_ref, o_ref):
    y = yf_ref[0, 0] + yb_ref[0, 0] + xs_ref[0] * dskip_ref[...]
    g = y * _silu(z_ref[0])
    for k in range(SSD_GROUPS):
        gs = slice(k * GROUP_W, (k + 1) * GROUP_W)
        gk = g[:, gs]
        ms = jnp.mean(gk * gk, axis=-1, keepdims=True)
        o_ref[0, :, gs] = ((gk * lax.rsqrt(ms + RMS_EPS)) * nw_ref[:, gs]).astype(o_ref.dtype)


def _ssd_finish(y2, xbc, p, d_wide, norm_w):
    _, bsz, length, _ = y2.shape
    return pl.pallas_call(
        _finish_kernel,
        grid=(bsz, length // ROW_TILE),
        in_specs=[pl.BlockSpec((1, 1, ROW_TILE, D_INNER), lambda b, i: (0, b, i, 0)),
                  pl.BlockSpec((1, 1, ROW_TILE, D_INNER), lambda b, i: (1, b, i, 0)),
                  pl.BlockSpec((1, ROW_TILE, D_INNER), lambda b, i: (b, i, 0)),
                  pl.BlockSpec((1, ROW_TILE, D_INNER), lambda b, i: (b, i, 0)),
                  pl.BlockSpec((1, D_INNER), lambda b, i: (0, 0)),
                  pl.BlockSpec((1, D_INNER), lambda b, i: (0, 0))],
        out_specs=pl.BlockSpec((1, ROW_TILE, D_INNER), lambda b, i: (b, i, 0)),
        out_shape=jax.ShapeDtypeStruct((bsz, length, D_INNER), BF16),
        compiler_params=_cparams(("parallel", "parallel")),
    )(y2, y2, xbc, p, d_wide, norm_w.reshape(1, D_INNER))


def _attn_kernel(q_ref, kc_ref, vc_ref, kp_ref, ko_ref, kn_ref, vp_ref, vo_ref, vn_ref,
                 sink_ref, o_ref, bias_ref, *, n_blocks):
    qb = pl.program_id(1)
    n_ctx = kc_ref.shape[1]
    blk = ATTN_BLOCK
    hd = ATTN_HEAD_DIM
    n_keys = n_ctx + 3 * blk

    @pl.when(jnp.logical_and(pl.program_id(0) == 0, qb == 0))
    def _():
        qi = lax.broadcasted_iota(jnp.int32, (blk, n_keys), 0)
        kj = lax.broadcasted_iota(jnp.int32, (blk, n_keys), 1) - n_ctx
        rel = kj - blk - qi
        in_window = jnp.logical_and(rel >= -blk, rel <= blk)
        for has_prev in (0, 1):
            for has_next in (0, 1):
                in_range = jnp.logical_and(jnp.logical_or(kj >= blk, has_prev == 1),
                                           jnp.logical_or(kj < 2 * blk, has_next == 1))
                ok = jnp.logical_or(kj < 0, jnp.logical_and(in_window, in_range))
                bias_ref[2 * has_prev + has_next] = jnp.where(ok, 0.0, -jnp.inf)

    case = jnp.where(qb > 0, 2, 0) + jnp.where(qb < n_blocks - 1, 1, 0)
    for k in range(ATTN_KV_HEADS):
        ks = slice(k * hd, (k + 1) * hd)
        keys = jnp.concatenate([kc_ref[0][:, ks], kp_ref[0][:, ks], ko_ref[0][:, ks], kn_ref[0][:, ks]], axis=0)
        vals = jnp.concatenate([vc_ref[0][:, ks], vp_ref[0][:, ks], vo_ref[0][:, ks], vn_ref[0][:, ks]], axis=0)
        heads = [k * ATTN_GROUP + g for g in range(ATTN_GROUP)]
        qs = jnp.concatenate([q_ref[0, :, h * hd:(h + 1) * hd] for h in heads], axis=0)
        sink = jnp.concatenate([jnp.full((blk, 1), sink_ref[h], F32) for h in heads], axis=0)
        s = lax.dot_general(qs, keys, (((1,), (1,)), ((), ())), preferred_element_type=F32)
        s = (s.reshape(ATTN_GROUP, blk, n_keys) + bias_ref[case][None]).reshape(ATTN_GROUP * blk, n_keys)
        m = jnp.maximum(jnp.max(s, axis=-1, keepdims=True), sink)
        e = jnp.exp(s - m)
        denom = jnp.sum(e, axis=-1, keepdims=True) + jnp.exp(sink - m)
        o = jnp.dot(e.astype(BF16), vals, preferred_element_type=F32) / denom
        for g, h in enumerate(heads):
            o_ref[0, :, h * hd:(h + 1) * hd] = o[g * blk:(g + 1) * blk].astype(o_ref.dtype)


def _attention(qkv, sinks, n_ctx):
    bsz, length, _ = qkv.shape
    n_lat = length - n_ctx
    nb = n_lat // ATTN_BLOCK
    cb = n_ctx // ATTN_BLOCK
    kcol = Q_DIM // KV_DIM
    vcol = kcol + 1

    def band(col, off):
        return pl.BlockSpec(
            (1, ATTN_BLOCK, KV_DIM),
            lambda b, i: (b, cb + jnp.clip(i + off, 0, nb - 1), col))

    return pl.pallas_call(
        functools.partial(_attn_kernel, n_blocks=nb),
        grid=(bsz, nb),
        in_specs=[pl.BlockSpec((1, ATTN_BLOCK, Q_DIM), lambda b, i: (b, cb + i, 0)),
                  pl.BlockSpec((1, n_ctx, KV_DIM), lambda b, i: (b, 0, kcol)),
                  pl.BlockSpec((1, n_ctx, KV_DIM), lambda b, i: (b, 0, vcol)),
                  band(kcol, -1), band(kcol, 0), band(kcol, 1),
                  band(vcol, -1), band(vcol, 0), band(vcol, 1),
                  pl.BlockSpec(memory_space=pltpu.SMEM)],
        out_specs=pl.BlockSpec((1, ATTN_BLOCK, Q_DIM), lambda b, i: (b, i, 0)),
        out_shape=jax.ShapeDtypeStruct((bsz, n_lat, Q_DIM), BF16),
        scratch_shapes=[pltpu.VMEM((4, ATTN_BLOCK, n_ctx + 3 * ATTN_BLOCK), F32)],
        compiler_params=_cparams(("arbitrary", "arbitrary")),
    )(qkv, qkv, qkv, qkv, qkv, qkv, qkv, qkv, qkv, sinks)


def _layer_norm(v, g, b):
    mu = jnp.mean(v, axis=-1, keepdims=True)
    cen = v - mu
    var = jnp.mean(cen * cen, axis=-1, keepdims=True)
    return cen * lax.rsqrt(var + LN_EPS) * g + b


def _proj_ln_kernel(a_ref, w_ref, bo_ref, x_ref, mod_ref, lng_ref, lnb_ref, wr_ref, br_ref,
                    x1_ref, tok_ref, ti_ref, tw_ref, rank_ref, cnt_ref, carry_ref):
    first = jnp.logical_and(pl.program_id(0) == 0, pl.program_id(1) == 0)

    @pl.when(first)
    def _():
        carry_ref[...] = jnp.zeros_like(carry_ref)

    m = mod_ref[0, 0]
    y = jnp.dot(a_ref[0], w_ref[...], preferred_element_type=F32) + bo_ref[...]
    x1 = _layer_norm(ALPHA * x_ref[0] + m[2:3] * y, lng_ref[...], lnb_ref[...])
    x1_ref[0] = x1
    tok = x1 * (1.0 + m[4:5]) + m[3:4]
    tok_ref[0] = tok
    logits = jnp.dot(tok, wr_ref[...], precision=HIGHEST, preferred_element_type=F32) + br_ref[...]
    lane = lax.broadcasted_iota(jnp.int32, logits.shape, 1)
    vals = logits
    top_v, top_i = [], []
    for _ in range(TOP_K):
        mx = jnp.max(vals, axis=-1, keepdims=True)
        idx = jnp.min(jnp.where(vals == mx, lane, N_EXPERTS), axis=-1, keepdims=True)
        vals = jnp.where(lane == idx, -jnp.inf, vals)
        top_v.append(mx)
        top_i.append(idx)
    ex = [jnp.exp(v - top_v[0]) for v in top_v]
    den = ex[0] + ex[1] + ex[2] + ex[3]
    ti_ref[0] = jnp.concatenate(top_i, axis=1)
    tw_ref[0] = jnp.concatenate([e / den for e in ex], axis=1)

    rows = logits.shape[0]
    picked = [lane == idx for idx in top_i]
    onehot = jnp.where(picked[0] | picked[1] | picked[2] | picked[3], 1.0, 0.0)
    ri = lax.broadcasted_iota(jnp.int32, (rows, rows), 0)
    ci = lax.broadcasted_iota(jnp.int32, (rows, rows), 1)
    earlier = jnp.where(ci < ri, 1.0, 0.0).astype(BF16)
    before = jnp.dot(earlier, onehot.astype(BF16), preferred_element_type=F32) + carry_ref[0:1, :]
    ranks = [jnp.sum(jnp.where(p, before, 0.0), axis=-1, keepdims=True) for p in picked]
    rank_ref[0] = jnp.concatenate(ranks, axis=1).astype(jnp.int32)
    total = carry_ref[0:1, :] + jnp.sum(onehot, axis=0, keepdims=True)
    carry_ref[...] = jnp.broadcast_to(total, carry_ref.shape)
    cnt_ref[...] = jnp.broadcast_to(total, cnt_ref.shape).astype(jnp.int32)


def _proj_ln(a, w_bf, b_o, xs, mods, ln_g, ln_b, w_r, b_r, *, row_off_t, n_ctx_t):
    bsz, rows, kdim = a.shape
    d = xs.shape[2]

    def mod_idx(b, i):
        return (b, jnp.where(i + row_off_t >= n_ctx_t, 1, 0), 0, 0)

    row_spec = lambda width: pl.BlockSpec((1, ROW_TILE, width), lambda b, i: (b, i, 0))
    vec_spec = pl.BlockSpec((1, d), lambda b, i: (0, 0))
    return pl.pallas_call(
        _proj_ln_kernel,
        grid=(bsz, rows // ROW_TILE),
        in_specs=[row_spec(kdim),
                  pl.BlockSpec((kdim, d), lambda b, i: (0, 0)),
                  vec_spec,
                  pl.BlockSpec((1, ROW_TILE, d), lambda b, i: (b, i + row_off_t, 0)),
                  pl.BlockSpec((1, 1, 6, d), mod_idx),
                  vec_spec, vec_spec,
                  pl.BlockSpec((d, N_EXPERTS), lambda b, i: (0, 0)),
                  pl.BlockSpec((1, N_EXPERTS), lambda b, i: (0, 0))],
        out_specs=[row_spec(d), row_spec(d), row_spec(TOP_K), row_spec(TOP_K), row_spec(TOP_K),
                   pl.BlockSpec((SUBLANES, N_EXPERTS), lambda b, i: (0, 0))],
        out_shape=[jax.ShapeDtypeStruct((bsz, rows, d), F32),
                   jax.ShapeDtypeStruct((bsz, rows, d), F32),
                   jax.ShapeDtypeStruct((bsz, rows, TOP_K), jnp.int32),
                   jax.ShapeDtypeStruct((bsz, rows, TOP_K), F32),
                   jax.ShapeDtypeStruct((bsz, rows, TOP_K), jnp.int32),
                   jax.ShapeDtypeStruct((SUBLANES, N_EXPERTS), jnp.int32)],
        scratch_shapes=[pltpu.VMEM((SUBLANES, N_EXPERTS), F32)],
        compiler_params=_cparams(("arbitrary", "arbitrary")),
    )(a, w_bf, b_o.reshape(1, d), xs, mods, ln_g.reshape(1, d), ln_b.reshape(1, d),
      w_r, b_r.reshape(1, N_EXPERTS))


def _dispatch_kernel(last_ref, used_ref, free_ref, slot_ref, tok_ref, xg_ref, zero_ref, sem):
    rows = tok_ref.shape[0]
    n_tiles = xg_ref.shape[0] // MOE_TILE

    @pl.when(pl.program_id(0) == 0)
    def _():
        zero_ref[...] = jnp.zeros_like(zero_ref)

        def tile_fill(first_row):
            start = pl.multiple_of(first_row, MOE_TILE)
            return pltpu.make_async_copy(zero_ref, xg_ref.at[pl.ds(start, MOE_TILE)], sem)

        for e in range(N_EXPERTS):
            @pl.when(used_ref[e] > 0)
            def _():
                tile_fill(last_ref[e]).start()

        def start_free(t, carry):
            tile_fill(t * MOE_TILE).start()
            return carry

        lax.fori_loop(free_ref[0], n_tiles, start_free, 0)

        for e in range(N_EXPERTS):
            @pl.when(used_ref[e] > 0)
            def _():
                tile_fill(last_ref[e]).wait()

        def wait_free(t, carry):
            tile_fill(t * MOE_TILE).wait()
            return carry

        lax.fori_loop(free_ref[0], n_tiles, wait_free, 0)

    def row_copy(r, k):
        return pltpu.make_async_copy(tok_ref.at[pl.ds(r, 1)], xg_ref.at[pl.ds(slot_ref[0, 0, r * TOP_K + k], 1)], sem)

    def issue(r, carry):
        for k in range(TOP_K):
            row_copy(r, k).start()
        return carry

    lax.fori_loop(0, rows, issue, 0)

    def drain(r, carry):
        for k in range(TOP_K):
            row_copy(r, k).wait()
        return carry

    lax.fori_loop(0, rows, drain, 0)


def _dispatch(last_tile, used, first_free, slots, tok, n_slots):
    t, d = tok.shape
    nt = t // ROW_TILE
    grid_spec = pltpu.PrefetchScalarGridSpec(
        num_scalar_prefetch=3,
        grid=(nt,),
        in_specs=[pl.BlockSpec((1, 1, ROW_TILE * TOP_K), lambda i, lt, us, ff: (i, 0, 0), memory_space=pltpu.SMEM),
                  pl.BlockSpec((ROW_TILE, d), lambda i, lt, us, ff: (i, 0))],
        out_specs=pl.BlockSpec(memory_space=pl.ANY),
        scratch_shapes=[pltpu.VMEM((MOE_TILE, d), F32), pltpu.SemaphoreType.DMA(())],
    )
    return pl.pallas_call(
        _dispatch_kernel,
        grid_spec=grid_spec,
        out_shape=jax.ShapeDtypeStruct((n_slots, d), F32),
        compiler_params=pltpu.CompilerParams(dimension_semantics=("arbitrary",), vmem_limit_bytes=VMEM_LIMIT,
                                             disable_bounds_checks=True),
    )(last_tile, used, first_free, slots, tok)


def _expert_kernel(te_ref, tv_ref, tf_ref, x_ref, w1_ref, b1_ref, w2_ref, b2_ref, y_ref,
                   w1b_ref, w2f_ref, w2b_ref):
    i = pl.program_id(0)
    f2 = w1_ref.shape[3]
    prep_rows = 128

    @pl.when(i == 0)
    def _():
        w2f_ref[...] = jnp.zeros_like(w2f_ref)

    @pl.when(tf_ref[i] > 0)
    def _():
        def cast_w1(j, carry):
            r = pl.multiple_of(j * prep_rows, prep_rows)
            w1b_ref[pl.ds(r, prep_rows), :] = w1_ref[0, 0, pl.ds(r, prep_rows), :].astype(BF16)
            return carry

        lax.fori_loop(0, w1_ref.shape[2] // prep_rows, cast_w1, 0)
        for c in range(w2_ref.shape[3] // LANES):
            cs = slice(c * LANES, (c + 1) * LANES)
            w2f_ref[c, pl.ds(0, f2 // 2, stride=2), :] = w2_ref[0, 0, :, cs]
            w2b_ref[:, cs] = w2f_ref[c].astype(BF16)

    @pl.when(tv_ref[i] > 0)
    def _():
        x = x_ref[...].astype(BF16)
        even = lax.broadcasted_iota(jnp.int32, (x.shape[0], LANES), 1) % 2 == 0
        acts = []
        for c0 in range(0, f2, MXU_COLS):
            wide = jnp.dot(x, w1b_ref[:, c0:c0 + MXU_COLS], preferred_element_type=F32) + b1_ref[0, 0][:, c0:c0 + MXU_COLS]
            for l0 in range(0, MXU_COLS, LANES):
                h = wide[:, l0:l0 + LANES]
                glu = jnp.minimum(h, SWIGLU_LIMIT)
                lin = jnp.clip(h, -SWIGLU_LIMIT, SWIGLU_LIMIT)
                gate = glu * (1.0 / (1.0 + jnp.exp(-SWIGLU_ALPHA * glu)))
                nxt = pltpu.roll(lin, LANES - 1, axis=1)
                acts.append(jnp.where(even, gate * (nxt + 1.0), 0.0).astype(BF16))
        act = jnp.concatenate(acts, axis=1)
        y_ref[...] = jnp.dot(act, w2b_ref[...], preferred_element_type=F32) + b2_ref[0, 0]

    @pl.when(tv_ref[i] == 0)
    def _():
        y_ref[...] = jnp.zeros_like(y_ref)


def _experts(tile_e, tile_v, tile_f, xg, w1, b1, w2, b2, layer):
    ns, d = xg.shape
    depth, n_e, _, f2 = w1.shape
    nt = ns // MOE_TILE
    grid_spec = pltpu.PrefetchScalarGridSpec(
        num_scalar_prefetch=3,
        grid=(nt,),
        in_specs=[pl.BlockSpec((MOE_TILE, d), lambda i, te, tv, tf: (jnp.where(tv[i] > 0, i, 0), 0)),
                  pl.BlockSpec((1, 1, d, f2), lambda i, te, tv, tf: (layer, te[i], 0, 0)),
                  pl.BlockSpec((1, 1, 1, f2), lambda i, te, tv, tf: (layer, te[i], 0, 0)),
                  pl.BlockSpec((1, 1, f2 // 2, d), lambda i, te, tv, tf: (layer, te[i], 0, 0)),
                  pl.BlockSpec((1, 1, 1, d), lambda i, te, tv, tf: (layer, te[i], 0, 0))],
        out_specs=pl.BlockSpec((MOE_TILE, d), lambda i, te, tv, tf: (i, 0)),
        scratch_shapes=[pltpu.VMEM((d, f2), BF16), pltpu.VMEM((d // LANES, f2, LANES), F32),
                        pltpu.VMEM((f2, d), BF16)],
    )
    return pl.pallas_call(
        _expert_kernel,
        grid_spec=grid_spec,
        out_shape=jax.ShapeDtypeStruct((ns, d), F32),
        compiler_params=_cparams(("arbitrary",)),
    )(tile_e, tile_v, tile_f, xg, w1, b1.reshape(depth, n_e, 1, f2), w2, b2.reshape(depth, n_e, 1, d))


def _combine_kernel(slot_ref, next_slot_ref, tw_ref, x_ref, mod_ref, lng_ref, lnb_ref, y_ref, o_ref, buf_ref, sem):
    rows = x_ref.shape[1]
    step = pl.program_id(0) * pl.num_programs(1) + pl.program_id(1)
    n_steps = pl.num_programs(0) * pl.num_programs(1)
    cur = step % 2

    def row_copy(slots, half, r, k):
        return pltpu.make_async_copy(y_ref.at[pl.ds(slots[0, 0, r * TOP_K + k], 1)],
                                     buf_ref.at[half, k, pl.ds(r, 1)], sem.at[half])

    def gather(slots, half):
        def issue(r, carry):
            for k in range(TOP_K):
                row_copy(slots, half, r, k).start()
            return carry

        lax.fori_loop(0, rows, issue, 0)

    @pl.when(step == 0)
    def _():
        gather(slot_ref, cur)

    @pl.when(step + 1 < n_steps)
    def _():
        gather(next_slot_ref, 1 - cur)

    def drain(r, carry):
        for k in range(TOP_K):
            row_copy(slot_ref, cur, r, k).wait()
        return carry

    lax.fori_loop(0, rows, drain, 0)

    m = mod_ref[0, 0]
    tw = tw_ref[0]
    f = tw[:, 0:1] * buf_ref[cur, 0]
    for k in range(1, TOP_K):
        f = f + tw[:, k:k + 1] * buf_ref[cur, k]
    o_ref[0] = _layer_norm(ALPHA * x_ref[0] + m[5:6] * f, lng_ref[...], lnb_ref[...])


def _combine(slots, y, tw, x1, mods, ln_g, ln_b, *, row_off_t, n_ctx_t):
    bsz, rows, d = x1.shape
    n_t = rows // ROW_TILE

    def mod_idx(b, i):
        return (b, jnp.where(i + row_off_t >= n_ctx_t, 1, 0), 0, 0)

    vec_spec = pl.BlockSpec((1, d), lambda b, i: (0, 0))
    last_step = bsz * n_t - 1
    slot_spec = lambda ahead: pl.BlockSpec((1, 1, ROW_TILE * TOP_K),
                                           lambda b, i: (jnp.minimum(b * n_t + i + ahead, last_step), 0, 0),
                                           memory_space=pltpu.SMEM)
    return pl.pallas_call(
        _combine_kernel,
        grid=(bsz, n_t),
        in_specs=[slot_spec(0), slot_spec(1),
                  pl.BlockSpec((1, ROW_TILE, TOP_K), lambda b, i: (b, i, 0)),
                  pl.BlockSpec((1, ROW_TILE, d), lambda b, i: (b, i, 0)),
                  pl.BlockSpec((1, 1, 6, d), mod_idx),
                  vec_spec, vec_spec,
                  pl.BlockSpec(memory_space=pl.ANY)],
        out_specs=pl.BlockSpec((1, ROW_TILE, d), lambda b, i: (b, i, 0)),
        out_shape=jax.ShapeDtypeStruct((bsz, rows, d), F32),
        scratch_shapes=[pltpu.VMEM((2, TOP_K, ROW_TILE, d), F32), pltpu.SemaphoreType.DMA((2,))],
        compiler_params=pltpu.CompilerParams(dimension_semantics=("arbitrary", "arbitrary"),
                                             vmem_limit_bytes=VMEM_LIMIT, disable_bounds_checks=True),
    )(slots, slots, tw, x1, mods, ln_g.reshape(1, d), ln_b.reshape(1, d), y)


def _moe_layout(top_i, rank, counts):
    n_assign = top_i.size
    padded = ((counts + MOE_TILE - 1) // MOE_TILE) * MOE_TILE
    pad_end = jnp.cumsum(padded)
    pad_off = pad_end - padded
    experts = jnp.arange(N_EXPERTS, dtype=jnp.int32)
    base = jnp.sum(jnp.where(top_i[..., None] == experts, pad_off, 0), axis=-1)
    slots = (base + rank).astype(jnp.int32).reshape(n_assign // (ROW_TILE * TOP_K), 1, ROW_TILE * TOP_K)
    n_tiles = n_assign // MOE_TILE + N_EXPERTS
    tile_start = jnp.arange(n_tiles, dtype=jnp.int32) * MOE_TILE
    owner = jnp.sum((tile_start[:, None] >= pad_end[None, :]).astype(jnp.int32), axis=1)
    tile_v = (owner < N_EXPERTS).astype(jnp.int32)
    tile_e = jnp.minimum(owner, N_EXPERTS - 1)
    used = (padded > 0).astype(jnp.int32)
    tile_f = tile_v * jnp.sum(((tile_start[:, None] == pad_off[None, :]) & (padded[None, :] > 0)).astype(jnp.int32),
                              axis=1)
    last_tile = jnp.maximum(pad_end - MOE_TILE, 0).astype(jnp.int32)
    first_free = (pad_end[N_EXPERTS - 1:] // MOE_TILE).astype(jnp.int32)
    return slots, tile_e, tile_v, tile_f, (last_tile, used, first_free), n_tiles * MOE_TILE


def _rope_tables(n_lat, n_ctx):
    rows = n_lat // GRID_W
    row = jnp.repeat(jnp.arange(rows), GRID_W).astype(F32)
    col = jnp.broadcast_to(jnp.arange(GRID_W)[None, :], (rows, GRID_W)).reshape(-1).astype(F32)
    n_freq = ATTN_HEAD_DIM // 4
    inv = jnp.power(ROPE_BASE, -jnp.arange(n_freq, dtype=F32) / n_freq)
    ang = jnp.concatenate([row[:, None] * inv, col[:, None] * inv], axis=-1)
    cos, sin = jnp.cos(ang), jnp.sin(ang)
    cos_h = jnp.concatenate([cos, cos], axis=-1)
    sin_h = jnp.concatenate([-sin, sin], axis=-1)
    reps = LANES // ATTN_HEAD_DIM
    cos_t = jnp.concatenate([jnp.ones((n_ctx, LANES), F32), jnp.tile(cos_h, (1, reps))], axis=0)
    sin_t = jnp.concatenate([jnp.zeros((n_ctx, LANES), F32), jnp.tile(sin_h, (1, reps))], axis=0)
    return cos_t, sin_t


def kernel(x, c, ctx, c_ctx, ada_w, ada_b, ln_g, ln_b, ssd_w_in, ssd_conv_w, ssd_conv_b, ssd_dt_bias,
           ssd_a_log, ssd_d, ssd_norm_w, ssd_w_out, attn_w_qkv, attn_b_qkv, attn_sinks, attn_w_o, attn_b_o,
           router_w, router_b, moe_w1, moe_b1, moe_w2, moe_b2):
    bsz, n_lat, d = x.shape
    n_ctx = ctx.shape[1]
    assert d == D_MODEL and bsz + 1 <= SUBLANES
    assert n_ctx % ROW_TILE == 0 and n_lat % ROW_TILE == 0 and n_lat % GRID_W == 0
    n_ctx_t = n_ctx // ROW_TILE

    cond = jnp.concatenate([c, c_ctx[None], jnp.zeros((SUBLANES - bsz - 1, d), F32)], axis=0)
    ada = _ada(cond, ada_w, ada_b)
    xs = jnp.concatenate([ctx, x], axis=1)

    for i in range(DEPTH):
        last = i == DEPTH - 1
        m_lat = ada[i, :bsz].reshape(bsz, 1, 6, d)
        m_ctx = jnp.broadcast_to(ada[i, bsz].reshape(1, 1, 6, d), (bsz, 1, 6, d))
        mods = jnp.concatenate([m_ctx, m_lat], axis=1)
        j = i // 2
        if i % 2 == 0:
            p = _ssd_inproj(xs, mods, ssd_w_in[j].astype(BF16), n_ctx_t)
            xbc = _ssd_conv(p, ssd_conv_w[j], ssd_conv_b[j], n_ctx_t)
            y2 = _ssd_scan(xbc, p, ssd_dt_bias[j], ssd_a_log[j], n_ctx // CHUNK)
            d_wide = jnp.repeat(ssd_d[j], SSD_HEAD_DIM).reshape(1, D_INNER)
            mix = _ssd_finish(y2, xbc, p, d_wide, ssd_norm_w[j])
            w_o, b_o = ssd_w_out[j].astype(BF16), jnp.zeros((d,), F32)
            if last:
                mix = mix[:, n_ctx:]
        else:
            cos_t, sin_t = _rope_tables(n_lat, n_ctx)
            qkv = _attn_qkv(xs, mods, attn_w_qkv[j].astype(BF16), attn_b_qkv[j], cos_t, sin_t, n_ctx_t)
            mix = _attention(qkv, attn_sinks[j].astype(F32), n_ctx)
            w_o, b_o = attn_w_o[j].astype(BF16), attn_b_o[j]
            if not last:
                raise NotImplementedError("context queries are only needed when an attention layer is not last")
        row_off_t = n_ctx_t if last else 0
        x1, tok, top_i, top_w, rank, counts = _proj_ln(mix, w_o, b_o, xs, mods, ln_g[i, 0], ln_b[i, 0],
                                                       router_w[i], router_b[i], row_off_t=row_off_t, n_ctx_t=n_ctx_t)
        slots, tile_e, tile_v, tile_f, fill, n_slots = _moe_layout(top_i, rank, counts[0])
        xg = _dispatch(*fill, slots, tok.reshape(-1, d), n_slots)
        y = _experts(tile_e, tile_v, tile_f, xg, moe_w1, moe_b1, moe_w2, moe_b2, i)
        xs = _combine(slots, y, top_w, x1, mods, ln_g[i, 1], ln_b[i, 1], row_off_t=row_off_t, n_ctx_t=n_ctx_t)
    return xs
```

```python
import functools

import jax
import jax.numpy as jnp
from jax import lax
from jax.experimental import pallas as pl
from jax.experimental.pallas import tpu as pltpu

F32 = jnp.float32
BF16 = jnp.bfloat16
HIGHEST = lax.Precision.HIGHEST

D_MODEL = 1024
DEPTH = 2
GRID_W = 64
ALPHA = (2.0 * DEPTH) ** 0.25
LN_EPS = 1e-5
RMS_EPS = 1e-5

D_INNER = 2 * D_MODEL
SSD_HEAD_DIM = 64
SSD_HEADS = D_INNER // SSD_HEAD_DIM
SSD_GROUPS = 4
SSD_HPG = SSD_HEADS // SSD_GROUPS
D_STATE = 128
GROUP_W = SSD_HPG * SSD_HEAD_DIM
CONV_DIM = D_INNER + 2 * SSD_GROUPS * D_STATE
SSD_IN_DIM = D_INNER + CONV_DIM + 2 * SSD_HEADS
DT_COL = D_INNER + CONV_DIM
CHUNK = 128

ATTN_HEADS = 16
ATTN_KV_HEADS = 4
ATTN_GROUP = ATTN_HEADS // ATTN_KV_HEADS
ATTN_HEAD_DIM = 64
Q_DIM = ATTN_HEADS * ATTN_HEAD_DIM
KV_DIM = ATTN_KV_HEADS * ATTN_HEAD_DIM
QKV_DIM = Q_DIM + 2 * KV_DIM
ATTN_BLOCK = 128
ROPE_BASE = 10000.0

N_EXPERTS = 32
TOP_K = 4
D_FF = D_MODEL
SWIGLU_ALPHA = 1.702
SWIGLU_LIMIT = 7.0

LANES = 128
SUBLANES = 8
MXU_COLS = 256
ROW_TILE = 256
MOE_TILE = 256
VMEM_LIMIT = 56 * 1024 * 1024


def _cparams(sem):
    return pltpu.CompilerParams(dimension_semantics=sem, vmem_limit_bytes=VMEM_LIMIT)


def _silu(v):
    return v * (1.0 / (1.0 + jnp.exp(-v)))


def _softplus(v):
    return jnp.maximum(v, 0.0) + jnp.log1p(jnp.exp(-jnp.abs(v)))


def _pack_bf16_pairs(v):
    half = v.shape[1] // 2
    hi = lax.bitcast_convert_type(v[:, :half].astype(BF16).astype(F32), jnp.uint32)
    lo = lax.bitcast_convert_type(v[:, half:].astype(BF16).astype(F32), jnp.uint32)
    return hi | (lo >> 16)


def _unpack_bf16_pairs(p):
    hi = lax.bitcast_convert_type(p & jnp.uint32(0xFFFF0000), F32)
    lo = lax.bitcast_convert_type(p << 16, F32)
    return jnp.concatenate([hi, lo], axis=1)


def _ada_kernel(c_ref, w_ref, b_ref, o_ref):
    a = _silu(c_ref[...])
    o_ref[0] = jnp.dot(a, w_ref[0], precision=HIGHEST, preferred_element_type=F32) + b_ref[0]


def _ada(cond, ada_w, ada_b):
    depth, d, n = ada_w.shape
    tn = 1536
    return pl.pallas_call(
        _ada_kernel,
        grid=(depth, n // tn),
        in_specs=[pl.BlockSpec((SUBLANES, d), lambda l, j: (0, 0)),
                  pl.BlockSpec((1, d, tn), lambda l, j: (l, 0, j)),
                  pl.BlockSpec((1, 1, tn), lambda l, j: (l, 0, j))],
        out_specs=pl.BlockSpec((1, SUBLANES, tn), lambda l, j: (l, 0, j)),
        out_shape=jax.ShapeDtypeStruct((depth, SUBLANES, n), F32),
        compiler_params=_cparams(("parallel", "parallel")),
    )(cond, ada_w, ada_b.reshape(depth, 1, n))


def _inproj_kernel(x_ref, mod_ref, w_ref, o_ref, *, col_chunk):
    m = mod_ref[0, 0]
    h = (x_ref[0] * (1.0 + m[1:2]) + m[0:1]).astype(BF16)
    n = w_ref.shape[1]
    for c0 in range(0, n, col_chunk):
        c1 = min(c0 + col_chunk, n)
        o_ref[0, :, c0:c1] = jnp.dot(h, w_ref[:, c0:c1], preferred_element_type=F32)


def _ssd_inproj(xs, mods, w_bf, n_ctx_t):
    bsz, length, d = xs.shape
    n = w_bf.shape[1]
    return pl.pallas_call(
        functools.partial(_inproj_kernel, col_chunk=1024),
        grid=(bsz, length // ROW_TILE),
        in_specs=[pl.BlockSpec((1, ROW_TILE, d), lambda b, i: (b, i, 0)),
                  pl.BlockSpec((1, 1, 6, d), lambda b, i: (b, jnp.where(i >= n_ctx_t, 1, 0), 0, 0)),
                  pl.BlockSpec((d, n), lambda b, i: (0, 0))],
        out_specs=pl.BlockSpec((1, ROW_TILE, n), lambda b, i: (b, i, 0)),
        out_shape=jax.ShapeDtypeStruct((bsz, length, n), F32),
        compiler_params=_cparams(("parallel", "parallel")),
    )(xs, mods, w_bf)


def _qkv_kernel(x_ref, mod_ref, w_ref, b_ref, cos_ref, sin_ref, o_ref):
    m = mod_ref[0, 0]
    h = (x_ref[0] * (1.0 + m[1:2]) + m[0:1]).astype(BF16)
    cosf = cos_ref[...]
    sinf = sin_ref[...]
    lane = lax.broadcasted_iota(jnp.int32, cosf.shape, 1)
    first_half = (lane % ATTN_HEAD_DIM) < (ATTN_HEAD_DIM // 2)
    scale = ATTN_HEAD_DIM ** -0.5
    n_rope = (Q_DIM + KV_DIM) // LANES
    for c0 in range(0, QKV_DIM, MXU_COLS):
        wide = jnp.dot(h, w_ref[:, c0:c0 + MXU_COLS], preferred_element_type=F32) + b_ref[:, c0:c0 + MXU_COLS]
        for c in range(c0 // LANES, (c0 + MXU_COLS) // LANES):
            t = wide[:, c * LANES - c0:(c + 1) * LANES - c0]
            if c < n_rope:
                swapped = jnp.where(first_half,
                                    pltpu.roll(t, LANES - ATTN_HEAD_DIM // 2, axis=1),
                                    pltpu.roll(t, ATTN_HEAD_DIM // 2, axis=1))
                t = t * cosf + swapped * sinf
            if c < Q_DIM // LANES:
                t = t * scale
            o_ref[0, :, c * LANES:(c + 1) * LANES] = t.astype(o_ref.dtype)


def _attn_qkv(xs, mods, w_bf, b, cos_t, sin_t, n_ctx_t):
    bsz, length, d = xs.shape
    n = w_bf.shape[1]
    return pl.pallas_call(
        _qkv_kernel,
        grid=(bsz, length // ROW_TILE),
        in_specs=[pl.BlockSpec((1, ROW_TILE, d), lambda b_, i: (b_, i, 0)),
                  pl.BlockSpec((1, 1, 6, d), lambda b_, i: (b_, jnp.where(i >= n_ctx_t, 1, 0), 0, 0)),
                  pl.BlockSpec((d, n), lambda b_, i: (0, 0)),
                  pl.BlockSpec((1, n), lambda b_, i: (0, 0)),
                  pl.BlockSpec((ROW_TILE, LANES), lambda b_, i: (i, 0)),
                  pl.BlockSpec((ROW_TILE, LANES), lambda b_, i: (i, 0))],
        out_specs=pl.BlockSpec((1, ROW_TILE, n), lambda b_, i: (b_, i, 0)),
        out_shape=jax.ShapeDtypeStruct((bsz, length, n), BF16),
        compiler_params=_cparams(("parallel", "parallel")),
    )(xs, mods, w_bf, b.reshape(1, n), cos_t, sin_t)


def _conv_kernel(u_ref, prev_ref, next_ref, w_ref, b_ref, o_ref, *, n_ctx_t, n_t):
    i = pl.program_id(1)
    u = u_ref[0]
    rows = u.shape[0]
    starts_segment = jnp.logical_or(i == 0, i == n_ctx_t)
    ends_segment = jnp.logical_or(i == n_ctx_t - 1, i == n_t - 1)
    before = jnp.where(starts_segment, 0.0, prev_ref[0, SUBLANES - 1:SUBLANES, :])
    after = jnp.where(ends_segment, 0.0, next_ref[0, 0:1, :])
    row = lax.broadcasted_iota(jnp.int32, u.shape, 0)
    up = jnp.where(row == 0, before, pltpu.roll(u, 1, axis=0))
    dn = jnp.where(row == rows - 1, after, pltpu.roll(u, rows - 1, axis=0))
    w = w_ref[...]
    v = up * w[0:1] + u * w[1:2] + dn * w[2:3] + b_ref[...]
    o_ref[0] = _silu(v)


def _ssd_conv(p, conv_w, conv_b, n_ctx_t):
    bsz, length, _ = p.shape
    n_t = length // ROW_TILE
    cw = 1024
    col0 = D_INNER // cw
    r8 = ROW_TILE // SUBLANES
    last8 = length // SUBLANES - 1
    return pl.pallas_call(
        functools.partial(_conv_kernel, n_ctx_t=n_ctx_t, n_t=n_t),
        grid=(bsz, n_t, CONV_DIM // cw),
        in_specs=[pl.BlockSpec((1, ROW_TILE, cw), lambda b, i, j: (b, i, col0 + j)),
                  pl.BlockSpec((1, SUBLANES, cw), lambda b, i, j: (b, jnp.maximum(i * r8 - 1, 0), col0 + j)),
                  pl.BlockSpec((1, SUBLANES, cw), lambda b, i, j: (b, jnp.minimum((i + 1) * r8, last8), col0 + j)),
                  pl.BlockSpec((3, cw), lambda b, i, j: (0, j)),
                  pl.BlockSpec((1, cw), lambda b, i, j: (0, j))],
        out_specs=pl.BlockSpec((1, ROW_TILE, cw), lambda b, i, j: (b, i, j)),
        out_shape=jax.ShapeDtypeStruct((bsz, length, CONV_DIM), F32),
        compiler_params=_cparams(("parallel", "parallel", "parallel")),
    )(p, p, p, conv_w, conv_b.reshape(1, CONV_DIM))


def _scan_kernel(x_ref, b_ref, c_ref, dt_ref, bias_ref, alog_ref, y_ref, state_ref):
    d = pl.program_id(1)
    c = pl.program_id(2)
    fwd = d == 0

    @pl.when(c == 0)
    def _():
        state_ref[...] = jnp.zeros_like(state_ref)

    h2 = 2 * SSD_HEADS
    dt_all = _softplus(dt_ref[0][:, :h2] + bias_ref[...])
    a_all = -jnp.exp(alog_ref[...])
    dt = jnp.where(fwd, dt_all[:, :SSD_HEADS], dt_all[:, SSD_HEADS:])
    a = jnp.where(fwd, a_all[:, :SSD_HEADS], a_all[:, SSD_HEADS:])
    da = dt * a

    li = lax.broadcasted_iota(jnp.int32, (CHUNK, CHUNK), 0)
    si = lax.broadcasted_iota(jnp.int32, (CHUNK, CHUNK), 1)
    tri = jnp.where(fwd, si - li, li - si) <= 0
    tri_f = tri.astype(F32)
    cs = jnp.dot(tri_f, da, precision=HIGHEST, preferred_element_type=F32)
    cs_t = lax.dot_general(da, tri_f, (((0,), (1,)), ((), ())), precision=HIGHEST,
                           preferred_element_type=F32)
    tot = jnp.sum(da, axis=0, keepdims=True)

    fac = jnp.concatenate([dt, jnp.exp(tot - cs), jnp.exp(cs),
                           jnp.broadcast_to(jnp.exp(tot), (SUBLANES, SSD_HEADS))], axis=0)
    hi = fac.astype(BF16)
    lo = (fac - hi.astype(F32)).astype(BF16)
    hh = lax.broadcasted_iota(jnp.int32, (SSD_HEADS, D_INNER), 0)
    ch = lax.broadcasted_iota(jnp.int32, (SSD_HEADS, D_INNER), 1)
    expand = jnp.where(ch // SSD_HEAD_DIM == hh, 1.0, 0.0).astype(BF16)
    wide = (jnp.dot(hi, expand, preferred_element_type=F32)
            + jnp.dot(lo, expand, preferred_element_type=F32))
    dt_w = wide[0:CHUNK]
    to_end_w = wide[CHUNK:2 * CHUNK]
    from_start_w = wide[2 * CHUNK:3 * CHUNK]
    chunk_decay_w = wide[3 * CHUNK:3 * CHUNK + 1]

    xdt = x_ref[0] * dt_w
    xdt_b = xdt.astype(BF16)
    xend_b = (xdt * to_end_w).astype(BF16)

    for g in range(SSD_GROUPS):
        gs = slice(g * GROUP_W, (g + 1) * GROUP_W)
        bg = b_ref[0][:, g * D_STATE:(g + 1) * D_STATE].astype(BF16)
        cg = c_ref[0][:, g * D_STATE:(g + 1) * D_STATE].astype(BF16)
        cb = lax.dot_general(cg, bg, (((1,), (1,)), ((), ())), preferred_element_type=F32)
        st = state_ref[g]
        y_off = jnp.dot(cg, st.astype(BF16), preferred_element_type=F32) * from_start_w[:, gs]
        for pair in range(SSD_HPG // 2):
            parts = []
            for k in range(2):
                h = g * SSD_HPG + pair * 2 + k
                diff = cs[:, h:h + 1] - cs_t[h:h + 1, :]
                decay = jnp.exp(jnp.where(tri, diff, -jnp.inf))
                lmat = (cb * decay).astype(BF16)
                parts.append(jnp.dot(lmat, xdt_b[:, h * SSD_HEAD_DIM:(h + 1) * SSD_HEAD_DIM],
                                     preferred_element_type=F32))
            c0 = pair * 2 * SSD_HEAD_DIM
            y_ref[0, 0, :, g * GROUP_W + c0:g * GROUP_W + c0 + 2 * SSD_HEAD_DIM] = (
                jnp.concatenate(parts, axis=1) + y_off[:, c0:c0 + 2 * SSD_HEAD_DIM])
        new = lax.dot_general(bg, xend_b[:, gs], (((0,), (0,)), ((), ())), preferred_element_type=F32)
        state_ref[g] = st * chunk_decay_w[:, gs] + new


def _ssd_scan(xbc, p, dt_bias, a_log, n_ctx_c):
    bsz, length, _ = xbc.shape
    nc = length // CHUNK
    h2 = 2 * SSD_HEADS

    def chunk_of(d, c):
        back = jnp.where(c < n_ctx_c, n_ctx_c - 1 - c, nc - 1 - (c - n_ctx_c))
        return jnp.where(d == 0, c, back)

    gn = SSD_GROUPS * D_STATE
    return pl.pallas_call(
        _scan_kernel,
        grid=(bsz, 2, nc),
        in_specs=[pl.BlockSpec((1, CHUNK, D_INNER), lambda b, d, c: (b, chunk_of(d, c), 0)),
                  pl.BlockSpec((1, CHUNK, gn), lambda b, d, c: (b, chunk_of(d, c), D_INNER // gn)),
                  pl.BlockSpec((1, CHUNK, gn), lambda b, d, c: (b, chunk_of(d, c), D_INNER // gn + 1)),
                  pl.BlockSpec((1, CHUNK, LANES), lambda b, d, c: (b, chunk_of(d, c), DT_COL // LANES)),
                  pl.BlockSpec((1, h2), lambda b, d, c: (0, 0)),
                  pl.BlockSpec((1, h2), lambda b, d, c: (0, 0))],
        out_specs=pl.BlockSpec((1, 1, CHUNK, D_INNER), lambda b, d, c: (d, b, chunk_of(d, c), 0)),
        out_shape=jax.ShapeDtypeStruct((2, bsz, length, D_INNER), F32),
        scratch_shapes=[pltpu.VMEM((SSD_GROUPS, D_STATE, GROUP_W), F32)],
        compiler_params=_cparams(("parallel", "arbitrary", "arbitrary")),
    )(xbc, xbc, xbc, p, dt_bias.reshape(1, h2), a_log.reshape(1, h2))


def _finish_kernel(yf_ref, yb_ref, xs_ref, z_ref, dskip_ref, nw_ref, o_ref):
    y = yf_ref[0, 0] + yb_ref[0, 0] + xs_ref[0] * dskip_ref[...]
    g = y * _silu(z_ref[0])
    for k in range(SSD_GROUPS):
        gs = slice(k * GROUP_W, (k + 1) * GROUP_W)
        gk = g[:, gs]
        ms = jnp.mean(gk * gk, axis=-1, keepdims=True)
        o_ref[0, :, gs] = ((gk * lax.rsqrt(ms + RMS_EPS)) * nw_ref[:, gs]).astype(o_ref.dtype)


def _ssd_finish(y2, xbc, p, d_wide, norm_w):
    _, bsz, length, _ = y2.shape
    return pl.pallas_call(
        _finish_kernel,
        grid=(bsz, length // ROW_TILE),
        in_specs=[pl.BlockSpec((1, 1, ROW_TILE, D_INNER), lambda b, i: (0, b, i, 0)),
                  pl.BlockSpec((1, 1, ROW_TILE, D_INNER), lambda b, i: (1, b, i, 0)),
                  pl.BlockSpec((1, ROW_TILE, D_INNER), lambda b, i: (b, i, 0)),
                  pl.BlockSpec((1, ROW_TILE, D_INNER), lambda b, i: (b, i, 0)),
                  pl.BlockSpec((1, D_INNER), lambda b, i: (0, 0)),
                  pl.BlockSpec((1, D_INNER), lambda b, i: (0, 0))],
        out_specs=pl.BlockSpec((1, ROW_TILE, D_INNER), lambda b, i: (b, i, 0)),
        out_shape=jax.ShapeDtypeStruct((bsz, length, D_INNER), BF16),
        compiler_params=_cparams(("parallel", "parallel")),
    )(y2, y2, xbc, p, d_wide, norm_w.reshape(1, D_INNER))


def _attn_kernel(q_ref, kc_ref, vc_ref, kp_ref, ko_ref, kn_ref, vp_ref, vo_ref, vn_ref,
                 sink_ref, o_ref, bias_ref, *, n_blocks):
    qb = pl.program_id(1)
    n_ctx = kc_ref.shape[1]
    blk = ATTN_BLOCK
    hd = ATTN_HEAD_DIM
    n_keys = n_ctx + 3 * blk

    @pl.when(jnp.logical_and(pl.program_id(0) == 0, qb == 0))
    def _():
        qi = lax.broadcasted_iota(jnp.int32, (blk, n_keys), 0)
        kj = lax.broadcasted_iota(jnp.int32, (blk, n_keys), 1) - n_ctx
        rel = kj - blk - qi
        in_window = jnp.logical_and(rel >= -blk, rel <= blk)
        for has_prev in (0, 1):
            for has_next in (0, 1):
                in_range = jnp.logical_and(jnp.logical_or(kj >= blk, has_prev == 1),
                                           jnp.logical_or(kj < 2 * blk, has_next == 1))
                ok = jnp.logical_or(kj < 0, jnp.logical_and(in_window, in_range))
                bias_ref[2 * has_prev + has_next] = jnp.where(ok, 0.0, -jnp.inf)

    case = jnp.where(qb > 0, 2, 0) + jnp.where(qb < n_blocks - 1, 1, 0)
    for k in range(ATTN_KV_HEADS):
        ks = slice(k * hd, (k + 1) * hd)
        keys = jnp.concatenate([kc_ref[0][:, ks], kp_ref[0][:, ks], ko_ref[0][:, ks], kn_ref[0][:, ks]], axis=0)
        vals = jnp.concatenate([vc_ref[0][:, ks], vp_ref[0][:, ks], vo_ref[0][:, ks], vn_ref[0][:, ks]], axis=0)
        heads = [k * ATTN_GROUP + g for g in range(ATTN_GROUP)]
        qs = jnp.concatenate([q_ref[0, :, h * hd:(h + 1) * hd] for h in heads], axis=0)
        sink = jnp.concatenate([jnp.full((blk, 1), sink_ref[h], F32) for h in heads], axis=0)
        s = lax.dot_general(qs, keys, (((1,), (1,)), ((), ())), preferred_element_type=F32)
        s = (s.reshape(ATTN_GROUP, blk, n_keys) + bias_ref[case][None]).reshape(ATTN_GROUP * blk, n_keys)
        m = jnp.maximum(jnp.max(s, axis=-1, keepdims=True), sink)
        e = jnp.exp(s - m)
        denom = jnp.sum(e, axis=-1, keepdims=True) + jnp.exp(sink - m)
        o = jnp.dot(e.astype(BF16), vals, preferred_element_type=F32) / denom
        for g, h in enumerate(heads):
            o_ref[0, :, h * hd:(h + 1) * hd] = o[g * blk:(g + 1) * blk].astype(o_ref.dtype)


def _attention(qkv, sinks, n_ctx):
    bsz, length, _ = qkv.shape
    n_lat = length - n_ctx
    nb = n_lat // ATTN_BLOCK
    cb = n_ctx // ATTN_BLOCK
    kcol = Q_DIM // KV_DIM
    vcol = kcol + 1

    def band(col, off):
        return pl.BlockSpec(
            (1, ATTN_BLOCK, KV_DIM),
            lambda b, i: (b, cb + jnp.clip(i + off, 0, nb - 1), col))

    return pl.pallas_call(
        functools.partial(_attn_kernel, n_blocks=nb),
        grid=(bsz, nb),
        in_specs=[pl.BlockSpec((1, ATTN_BLOCK, Q_DIM), lambda b, i: (b, cb + i, 0)),
                  pl.BlockSpec((1, n_ctx, KV_DIM), lambda b, i: (b, 0, kcol)),
                  pl.BlockSpec((1, n_ctx, KV_DIM), lambda b, i: (b, 0, vcol)),
                  band(kcol, -1), band(kcol, 0), band(kcol, 1),
                  band(vcol, -1), band(vcol, 0), band(vcol, 1),
                  pl.BlockSpec(memory_space=pltpu.SMEM)],
        out_specs=pl.BlockSpec((1, ATTN_BLOCK, Q_DIM), lambda b, i: (b, i, 0)),
        out_shape=jax.ShapeDtypeStruct((bsz, n_lat, Q_DIM), BF16),
        scratch_shapes=[pltpu.VMEM((4, ATTN_BLOCK, n_ctx + 3 * ATTN_BLOCK), F32)],
        compiler_params=_cparams(("arbitrary", "arbitrary")),
    )(qkv, qkv, qkv, qkv, qkv, qkv, qkv, qkv, qkv, sinks)


def _layer_norm(v, g, b):
    mu = jnp.mean(v, axis=-1, keepdims=True)
    cen = v - mu
    var = jnp.mean(cen * cen, axis=-1, keepdims=True)
    return cen * lax.rsqrt(var + LN_EPS) * g + b


def _proj_ln_kernel(a_ref, w_ref, bo_ref, x_ref, mod_ref, lng_ref, lnb_ref, wr_ref, br_ref,
                    x1_ref, tok_ref, ti_ref, tw_ref, rank_ref, cnt_ref, carry_ref):
    first = jnp.logical_and(pl.program_id(0) == 0, pl.program_id(1) == 0)

    @pl.when(first)
    def _():
        carry_ref[...] = jnp.zeros_like(carry_ref)

    m = mod_ref[0, 0]
    y = jnp.dot(a_ref[0], w_ref[...], preferred_element_type=F32) + bo_ref[...]
    x1 = _layer_norm(ALPHA * x_ref[0] + m[2:3] * y, lng_ref[...], lnb_ref[...])
    x1_ref[0] = x1
    tok = x1 * (1.0 + m[4:5]) + m[3:4]
    tok_ref[0] = _pack_bf16_pairs(tok)
    logits = jnp.dot(tok, wr_ref[...], precision=HIGHEST, preferred_element_type=F32) + br_ref[...]
    lane = lax.broadcasted_iota(jnp.int32, logits.shape, 1)
    vals = logits
    top_v, top_i = [], []
    for _ in range(TOP_K):
        mx = jnp.max(vals, axis=-1, keepdims=True)
        idx = jnp.min(jnp.where(vals == mx, lane, N_EXPERTS), axis=-1, keepdims=True)
        vals = jnp.where(lane == idx, -jnp.inf, vals)
        top_v.append(mx)
        top_i.append(idx)
    ex = [jnp.exp(v - top_v[0]) for v in top_v]
    den = ex[0] + ex[1] + ex[2] + ex[3]
    ti_ref[0] = jnp.concatenate(top_i, axis=1)
    tw_ref[0] = jnp.concatenate([e / den for e in ex], axis=1)

    rows = logits.shape[0]
    picked = [lane == idx for idx in top_i]
    onehot = jnp.where(picked[0] | picked[1] | picked[2] | picked[3], 1.0, 0.0)
    ri = lax.broadcasted_iota(jnp.int32, (rows, rows), 0)
    ci = lax.broadcasted_iota(jnp.int32, (rows, rows), 1)
    earlier = jnp.where(ci < ri, 1.0, 0.0).astype(BF16)
    before = jnp.dot(earlier, onehot.astype(BF16), preferred_element_type=F32) + carry_ref[0:1, :]
    ranks = [jnp.sum(jnp.where(p, before, 0.0), axis=-1, keepdims=True) for p in picked]
    rank_ref[0] = jnp.concatenate(ranks, axis=1).astype(jnp.int32)
    total = carry_ref[0:1, :] + jnp.sum(onehot, axis=0, keepdims=True)
    carry_ref[...] = jnp.broadcast_to(total, carry_ref.shape)
    cnt_ref[...] = jnp.broadcast_to(total, cnt_ref.shape).astype(jnp.int32)


def _proj_ln(a, w_bf, b_o, xs, mods, ln_g, ln_b, w_r, b_r, *, row_off_t, n_ctx_t):
    bsz, rows, kdim = a.shape
    d = xs.shape[2]

    def mod_idx(b, i):
        return (b, jnp.where(i + row_off_t >= n_ctx_t, 1, 0), 0, 0)

    row_spec = lambda width: pl.BlockSpec((1, ROW_TILE, width), lambda b, i: (b, i, 0))
    vec_spec = pl.BlockSpec((1, d), lambda b, i: (0, 0))
    return pl.pallas_call(
        _proj_ln_kernel,
        grid=(bsz, rows // ROW_TILE),
        in_specs=[row_spec(kdim),
                  pl.BlockSpec((kdim, d), lambda b, i: (0, 0)),
                  vec_spec,
                  pl.BlockSpec((1, ROW_TILE, d), lambda b, i: (b, i + row_off_t, 0)),
                  pl.BlockSpec((1, 1, 6, d), mod_idx),
                  vec_spec, vec_spec,
                  pl.BlockSpec((d, N_EXPERTS), lambda b, i: (0, 0)),
                  pl.BlockSpec((1, N_EXPERTS), lambda b, i: (0, 0))],
        out_specs=[row_spec(d), row_spec(d // 2), row_spec(TOP_K), row_spec(TOP_K), row_spec(TOP_K),
                   pl.BlockSpec((SUBLANES, N_EXPERTS), lambda b, i: (0, 0))],
        out_shape=[jax.ShapeDtypeStruct((bsz, rows, d), F32),
                   jax.ShapeDtypeStruct((bsz, rows, d // 2), jnp.uint32),
                   jax.ShapeDtypeStruct((bsz, rows, TOP_K), jnp.int32),
                   jax.ShapeDtypeStruct((bsz, rows, TOP_K), F32),
                   jax.ShapeDtypeStruct((bsz, rows, TOP_K), jnp.int32),
                   jax.ShapeDtypeStruct((SUBLANES, N_EXPERTS), jnp.int32)],
        scratch_shapes=[pltpu.VMEM((SUBLANES, N_EXPERTS), F32)],
        compiler_params=_cparams(("arbitrary", "arbitrary")),
    )(a, w_bf, b_o.reshape(1, d), xs, mods, ln_g.reshape(1, d), ln_b.reshape(1, d),
      w_r, b_r.reshape(1, N_EXPERTS))


def _dispatch_kernel(last_ref, used_ref, free_ref, slot_ref, tok_ref, xg_ref, zero_ref, sem):
    rows = tok_ref.shape[0]
    n_tiles = xg_ref.shape[0] // MOE_TILE

    @pl.when(pl.program_id(0) == 0)
    def _():
        zero_ref[...] = jnp.zeros_like(zero_ref)

        def tile_fill(first_row):
            start = pl.multiple_of(first_row, MOE_TILE)
            return pltpu.make_async_copy(zero_ref, xg_ref.at[pl.ds(start, MOE_TILE)], sem)

        for e in range(N_EXPERTS):
            @pl.when(used_ref[e] > 0)
            def _():
                tile_fill(last_ref[e]).start()

        def start_free(t, carry):
            tile_fill(t * MOE_TILE).start()
            return carry

        lax.fori_loop(free_ref[0], n_tiles, start_free, 0)

        for e in range(N_EXPERTS):
            @pl.when(used_ref[e] > 0)
            def _():
                tile_fill(last_ref[e]).wait()

        def wait_free(t, carry):
            tile_fill(t * MOE_TILE).wait()
            return carry

        lax.fori_loop(free_ref[0], n_tiles, wait_free, 0)

    def row_copy(r, k):
        return pltpu.make_async_copy(tok_ref.at[pl.ds(r, 1)], xg_ref.at[pl.ds(slot_ref[0, 0, r * TOP_K + k], 1)], sem)

    def issue(r, carry):
        for k in range(TOP_K):
            row_copy(r, k).start()
        return carry

    lax.fori_loop(0, rows, issue, 0)

    def drain(r, carry):
        for k in range(TOP_K):
            row_copy(r, k).wait()
        return carry

    lax.fori_loop(0, rows, drain, 0)


def _dispatch(last_tile, used, first_free, slots, tok, n_slots):
    t, d = tok.shape
    nt = t // ROW_TILE
    grid_spec = pltpu.PrefetchScalarGridSpec(
        num_scalar_prefetch=3,
        grid=(nt,),
        in_specs=[pl.BlockSpec((1, 1, ROW_TILE * TOP_K), lambda i, lt, us, ff: (i, 0, 0), memory_space=pltpu.SMEM),
                  pl.BlockSpec((ROW_TILE, d), lambda i, lt, us, ff: (i, 0))],
        out_specs=pl.BlockSpec(memory_space=pl.ANY),
        scratch_shapes=[pltpu.VMEM((MOE_TILE, d), tok.dtype), pltpu.SemaphoreType.DMA(())],
    )
    return pl.pallas_call(
        _dispatch_kernel,
        grid_spec=grid_spec,
        out_shape=jax.ShapeDtypeStruct((n_slots, d), tok.dtype),
        compiler_params=pltpu.CompilerParams(dimension_semantics=("arbitrary",), vmem_limit_bytes=VMEM_LIMIT,
                                             disable_bounds_checks=True),
    )(last_tile, used, first_free, slots, tok)


def _expert_kernel(te_ref, tv_ref, tf_ref, x_ref, w1_ref, b1_ref, w2_ref, b2_ref, y_ref,
                   w1b_ref, w2f_ref, w2b_ref):
    i = pl.program_id(0)
    f2 = w1_ref.shape[3]
    prep_rows = 128

    @pl.when(i == 0)
    def _():
        w2f_ref[...] = jnp.zeros_like(w2f_ref)

    @pl.when(tf_ref[i] > 0)
    def _():
        def cast_w1(j, carry):
            r = pl.multiple_of(j * prep_rows, prep_rows)
            w1b_ref[pl.ds(r, prep_rows), :] = w1_ref[0, 0, pl.ds(r, prep_rows), :].astype(BF16)
            return carry

        lax.fori_loop(0, w1_ref.shape[2] // prep_rows, cast_w1, 0)
        for c in range(w2_ref.shape[3] // LANES):
            cs = slice(c * LANES, (c + 1) * LANES)
            w2f_ref[c, pl.ds(0, f2 // 2, stride=2), :] = w2_ref[0, 0, :, cs]
            w2b_ref[:, cs] = w2f_ref[c].astype(BF16)

    @pl.when(tv_ref[i] > 0)
    def _():
        x = _unpack_bf16_pairs(x_ref[...]).astype(BF16)
        even = lax.broadcasted_iota(jnp.int32, (x.shape[0], LANES), 1) % 2 == 0
        acts = []
        for c0 in range(0, f2, MXU_COLS):
            wide = jnp.dot(x, w1b_ref[:, c0:c0 + MXU_COLS], preferred_element_type=F32) + b1_ref[0, 0][:, c0:c0 + MXU_COLS]
            for l0 in range(0, MXU_COLS, LANES):
                h = wide[:, l0:l0 + LANES]
                glu = jnp.minimum(h, SWIGLU_LIMIT)
                lin = jnp.clip(h, -SWIGLU_LIMIT, SWIGLU_LIMIT)
                gate = glu * (1.0 / (1.0 + jnp.exp(-SWIGLU_ALPHA * glu)))
                nxt = pltpu.roll(lin, LANES - 1, axis=1)
                acts.append(jnp.where(even, gate * (nxt + 1.0), 0.0).astype(BF16))
        act = jnp.concatenate(acts, axis=1)
        y_ref[...] = _pack_bf16_pairs(jnp.dot(act, w2b_ref[...], preferred_element_type=F32) + b2_ref[0, 0])

    @pl.when(tv_ref[i] == 0)
    def _():
        y_ref[...] = jnp.zeros_like(y_ref)


def _experts(tile_e, tile_v, tile_f, xg, w1, b1, w2, b2, layer):
    ns, dp = xg.shape
    depth, n_e, d, f2 = w1.shape
    nt = ns // MOE_TILE
    grid_spec = pltpu.PrefetchScalarGridSpec(
        num_scalar_prefetch=3,
        grid=(nt,),
        in_specs=[pl.BlockSpec((MOE_TILE, dp), lambda i, te, tv, tf: (jnp.where(tv[i] > 0, i, 0), 0)),
                  pl.BlockSpec((1, 1, d, f2), lambda i, te, tv, tf: (layer, te[i], 0, 0)),
                  pl.BlockSpec((1, 1, 1, f2), lambda i, te, tv, tf: (layer, te[i], 0, 0)),
                  pl.BlockSpec((1, 1, f2 // 2, d), lambda i, te, tv, tf: (layer, te[i], 0, 0)),
                  pl.BlockSpec((1, 1, 1, d), lambda i, te, tv, tf: (layer, te[i], 0, 0))],
        out_specs=pl.BlockSpec((MOE_TILE, dp), lambda i, te, tv, tf: (i, 0)),
        scratch_shapes=[pltpu.VMEM((d, f2), BF16), pltpu.VMEM((d // LANES, f2, LANES), F32),
                        pltpu.VMEM((f2, d), BF16)],
    )
    return pl.pallas_call(
        _expert_kernel,
        grid_spec=grid_spec,
        out_shape=jax.ShapeDtypeStruct((ns, dp), jnp.uint32),
        compiler_params=_cparams(("arbitrary",)),
    )(tile_e, tile_v, tile_f, xg, w1, b1.reshape(depth, n_e, 1, f2), w2, b2.reshape(depth, n_e, 1, d))


def _combine_kernel(slot_ref, next_slot_ref, tw_ref, x_ref, mod_ref, lng_ref, lnb_ref, y_ref, o_ref, buf_ref, sem):
    rows = x_ref.shape[1]
    step = pl.program_id(0) * pl.num_programs(1) + pl.program_id(1)
    n_steps = pl.num_programs(0) * pl.num_programs(1)
    cur = step % 2

    def row_copy(slots, half, r, k):
        return pltpu.make_async_copy(y_ref.at[pl.ds(slots[0, 0, r * TOP_K + k], 1)],
                                     buf_ref.at[half, k, pl.ds(r, 1)], sem.at[half])

    def gather(slots, half):
        def issue(r, carry):
            for k in range(TOP_K):
                row_copy(slots, half, r, k).start()
            return carry

        lax.fori_loop(0, rows, issue, 0)

    @pl.when(step == 0)
    def _():
        gather(slot_ref, cur)

    @pl.when(step + 1 < n_steps)
    def _():
        gather(next_slot_ref, 1 - cur)

    def drain(r, carry):
        for k in range(TOP_K):
            row_copy(slot_ref, cur, r, k).wait()
        return carry

    lax.fori_loop(0, rows, drain, 0)

    m = mod_ref[0, 0]
    tw = tw_ref[0]
    f = tw[:, 0:1] * _unpack_bf16_pairs(buf_ref[cur, 0])
    for k in range(1, TOP_K):
        f = f + tw[:, k:k + 1] * _unpack_bf16_pairs(buf_ref[cur, k])
    o_ref[0] = _layer_norm(ALPHA * x_ref[0] + m[5:6] * f, lng_ref[...], lnb_ref[...])


def _combine(slots, y, tw, x1, mods, ln_g, ln_b, *, row_off_t, n_ctx_t):
    bsz, rows, d = x1.shape
    n_t = rows // ROW_TILE

    def mod_idx(b, i):
        return (b, jnp.where(i + row_off_t >= n_ctx_t, 1, 0), 0, 0)

    vec_spec = pl.BlockSpec((1, d), lambda b, i: (0, 0))
    last_step = bsz * n_t - 1
    slot_spec = lambda ahead: pl.BlockSpec((1, 1, ROW_TILE * TOP_K),
                                           lambda b, i: (jnp.minimum(b * n_t + i + ahead, last_step), 0, 0),
                                           memory_space=pltpu.SMEM)
    return pl.pallas_call(
        _combine_kernel,
        grid=(bsz, n_t),
        in_specs=[slot_spec(0), slot_spec(1),
                  pl.BlockSpec((1, ROW_TILE, TOP_K), lambda b, i: (b, i, 0)),
                  pl.BlockSpec((1, ROW_TILE, d), lambda b, i: (b, i, 0)),
                  pl.BlockSpec((1, 1, 6, d), mod_idx),
                  vec_spec, vec_spec,
                  pl.BlockSpec(memory_space=pl.ANY)],
        out_specs=pl.BlockSpec((1, ROW_TILE, d), lambda b, i: (b, i, 0)),
        out_shape=jax.ShapeDtypeStruct((bsz, rows, d), F32),
        scratch_shapes=[pltpu.VMEM((2, TOP_K, ROW_TILE, d // 2), jnp.uint32), pltpu.SemaphoreType.DMA((2,))],
        compiler_params=pltpu.CompilerParams(dimension_semantics=("arbitrary", "arbitrary"),
                                             vmem_limit_bytes=VMEM_LIMIT, disable_bounds_checks=True),
    )(slots, slots, tw, x1, mods, ln_g.reshape(1, d), ln_b.reshape(1, d), y)


def _moe_layout(top_i, rank, counts):
    n_assign = top_i.size
    padded = ((counts + MOE_TILE - 1) // MOE_TILE) * MOE_TILE
    pad_end = jnp.cumsum(padded)
    pad_off = pad_end - padded
    experts = jnp.arange(N_EXPERTS, dtype=jnp.int32)
    base = jnp.sum(jnp.where(top_i[..., None] == experts, pad_off, 0), axis=-1)
    slots = (base + rank).astype(jnp.int32).reshape(n_assign // (ROW_TILE * TOP_K), 1, ROW_TILE * TOP_K)
    n_tiles = n_assign // MOE_TILE + N_EXPERTS
    tile_start = jnp.arange(n_tiles, dtype=jnp.int32) * MOE_TILE
    owner = jnp.sum((tile_start[:, None] >= pad_end[None, :]).astype(jnp.int32), axis=1)
    tile_v = (owner < N_EXPERTS).astype(jnp.int32)
    tile_e = jnp.minimum(owner, N_EXPERTS - 1)
    used = (padded > 0).astype(jnp.int32)
    tile_f = tile_v * jnp.sum(((tile_start[:, None] == pad_off[None, :]) & (padded[None, :] > 0)).astype(jnp.int32),
                              axis=1)
    last_tile = jnp.maximum(pad_end - MOE_TILE, 0).astype(jnp.int32)
    first_free = (pad_end[N_EXPERTS - 1:] // MOE_TILE).astype(jnp.int32)
    return slots, tile_e, tile_v, tile_f, (last_tile, used, first_free), n_tiles * MOE_TILE


def _rope_tables(n_lat, n_ctx):
    rows = n_lat // GRID_W
    row = jnp.repeat(jnp.arange(rows), GRID_W).astype(F32)
    col = jnp.broadcast_to(jnp.arange(GRID_W)[None, :], (rows, GRID_W)).reshape(-1).astype(F32)
    n_freq = ATTN_HEAD_DIM // 4
    inv = jnp.power(ROPE_BASE, -jnp.arange(n_freq, dtype=F32) / n_freq)
    ang = jnp.concatenate([row[:, None] * inv, col[:, None] * inv], axis=-1)
    cos, sin = jnp.cos(ang), jnp.sin(ang)
    cos_h = jnp.concatenate([cos, cos], axis=-1)
    sin_h = jnp.concatenate([-sin, sin], axis=-1)
    reps = LANES // ATTN_HEAD_DIM
    cos_t = jnp.concatenate([jnp.ones((n_ctx, LANES), F32), jnp.tile(cos_h, (1, reps))], axis=0)
    sin_t = jnp.concatenate([jnp.zeros((n_ctx, LANES), F32), jnp.tile(sin_h, (1, reps))], axis=0)
    return cos_t, sin_t


def kernel(x, c, ctx, c_ctx, ada_w, ada_b, ln_g, ln_b, ssd_w_in, ssd_conv_w, ssd_conv_b, ssd_dt_bias,
           ssd_a_log, ssd_d, ssd_norm_w, ssd_w_out, attn_w_qkv, attn_b_qkv, attn_sinks, attn_w_o, attn_b_o,
           router_w, router_b, moe_w1, moe_b1, moe_w2, moe_b2):
    bsz, n_lat, d = x.shape
    n_ctx = ctx.shape[1]
    assert d == D_MODEL and bsz + 1 <= SUBLANES
    assert n_ctx % ROW_TILE == 0 and n_lat % ROW_TILE == 0 and n_lat % GRID_W == 0
    n_ctx_t = n_ctx // ROW_TILE

    cond = jnp.concatenate([c, c_ctx[None], jnp.zeros((SUBLANES - bsz - 1, d), F32)], axis=0)
    ada = _ada(cond, ada_w, ada_b)
    xs = jnp.concatenate([ctx, x], axis=1)

    for i in range(DEPTH):
        last = i == DEPTH - 1
        m_lat = ada[i, :bsz].reshape(bsz, 1, 6, d)
        m_ctx = jnp.broadcast_to(ada[i, bsz].reshape(1, 1, 6, d), (bsz, 1, 6, d))
        mods = jnp.concatenate([m_ctx, m_lat], axis=1)
        j = i // 2
        if i % 2 == 0:
            p = _ssd_inproj(xs, mods, ssd_w_in[j].astype(BF16), n_ctx_t)
            xbc = _ssd_conv(p, ssd_conv_w[j], ssd_conv_b[j], n_ctx_t)
            y2 = _ssd_scan(xbc, p, ssd_dt_bias[j], ssd_a_log[j], n_ctx // CHUNK)
            d_wide = jnp.repeat(ssd_d[j], SSD_HEAD_DIM).reshape(1, D_INNER)
            mix = _ssd_finish(y2, xbc, p, d_wide, ssd_norm_w[j])
            w_o, b_o = ssd_w_out[j].astype(BF16), jnp.zeros((d,), F32)
            if last:
                mix = mix[:, n_ctx:]
        else:
            cos_t, sin_t = _rope_tables(n_lat, n_ctx)
            qkv = _attn_qkv(xs, mods, attn_w_qkv[j].astype(BF16), attn_b_qkv[j], cos_t, sin_t, n_ctx_t)
            mix = _attention(qkv, attn_sinks[j].astype(F32), n_ctx)
            w_o, b_o = attn_w_o[j].astype(BF16), attn_b_o[j]
            if not last:
                raise NotImplementedError("context queries are only needed when an attention layer is not last")
        row_off_t = n_ctx_t if last else 0
        x1, tok, top_i, top_w, rank, counts = _proj_ln(mix, w_o, b_o, xs, mods, ln_g[i, 0], ln_b[i, 0],
                                                       router_w[i], router_b[i], row_off_t=row_off_t, n_ctx_t=n_ctx_t)
        slots, tile_e, tile_v, tile_f, fill, n_slots = _moe_layout(top_i, rank, counts[0])
        xg = _dispatch(*fill, slots, tok.reshape(-1, d // 2), n_slots)
        y = _experts(tile_e, tile_v, tile_f, xg, moe_w1, moe_b1, moe_w2, moe_b2, i)
        xs = _combine(slots, y, top_w, x1, mods, ln_g[i, 1], ln_b[i, 1], row_off_t=row_off_t, n_ctx_t=n_ctx_t)
    return xs
```
